```python
import math
import jax, jax.numpy as jnp
from jax import lax
import numpy as np

D_MODEL = 2048
BATCH = 4
SEQ = 4096
DEPTH = 4

N_EVEN = (DEPTH + 1) // 2
N_ODD = DEPTH // 2
RMS_EPS = 1e-6
LN_EPS = 1e-5

SSM_WIDTH = D_MODEL // 2
SSM_GROUP = 16
SSM_GROUPS = SSM_WIDTH // SSM_GROUP
SSM_STATE = 64
DT_MIN = 1e-3
DT_MAX = 1e-1

SG_WIDTH = D_MODEL // 2
SG_CHUNK = 128
SG_HEADS = 8
SG_HEAD_DIM = SG_WIDTH // SG_HEADS

EVEN_IN = 2 * SSM_WIDTH + 3 * SG_WIDTH
EVEN_MIX = SSM_WIDTH + SG_WIDTH

DA_HEADS = 8
DA_HEAD_DIM = D_MODEL // DA_HEADS // 2
DA_V_DIM = 2 * DA_HEAD_DIM
DA_WIDTH = DA_HEADS * DA_V_DIM
ODD_IN = 4 * DA_WIDTH
ROT_DIM = DA_HEAD_DIM // 4
ROPE_THETA = 500000.0
Q_BLOCK = 128

kernel_name = 'hybrid_s5_gmlp_diffattn_trunk'


def rmsnorm(x, g):
    xf = x.astype(jnp.float32)
    y = xf * lax.rsqrt(jnp.mean(xf * xf, axis=-1, keepdims=True) + RMS_EPS)
    return (y * g.astype(jnp.float32)).astype(x.dtype)


def layernorm(x, g, b):
    xf = x.astype(jnp.float32)
    mu = jnp.mean(xf, axis=-1, keepdims=True)
    xc = xf - mu
    y = xc * lax.rsqrt(jnp.mean(xc * xc, axis=-1, keepdims=True) + LN_EPS)
    return (y * g.astype(jnp.float32) + b.astype(jnp.float32)).astype(x.dtype)


def s5_mixer(u, lam_re, lam_im, log_dt, b_re, b_im, c_re, c_im, d_skip):
    f32 = jnp.float32
    dt = jnp.exp(log_dt.astype(f32))[:, None]
    lr = lam_re.astype(f32)
    li = lam_im.astype(f32)
    mag = jnp.exp(lr * dt)
    ab_re = mag * jnp.cos(li * dt)
    ab_im = mag * jnp.sin(li * dt)
    den = lr * lr + li * li
    nr = ab_re - 1.0
    f_re = (nr * lr + ab_im * li) / den
    f_im = (ab_im * lr - nr * li) / den
    br = b_re.astype(f32)
    bi = b_im.astype(f32)
    bb_re = f_re[..., None] * br - f_im[..., None] * bi
    bb_im = f_re[..., None] * bi + f_im[..., None] * br
    uf = u.astype(f32)
    bu_re = jnp.einsum('blgh,gph->blgp', uf, bb_re)
    bu_im = jnp.einsum('blgh,gph->blgp', uf, bb_im)
    seq = u.shape[1]
    a_re = jnp.broadcast_to(ab_re[None, None], (1, seq) + ab_re.shape)
    a_im = jnp.broadcast_to(ab_im[None, None], (1, seq) + ab_im.shape)

    def combine(e1, e2):
        a1r, a1i, b1r, b1i = e1
        a2r, a2i, b2r, b2i = e2
        return (a2r * a1r - a2i * a1i,
                a2r * a1i + a2i * a1r,
                a2r * b1r - a2i * b1i + b2r,
                a2r * b1i + a2i * b1r + b2i)

    _, _, h_re, h_im = lax.associative_scan(combine, (a_re, a_im, bu_re, bu_im), axis=1)
    y = (jnp.einsum('blgp,ghp->blgh', h_re, c_re.astype(f32))
         - jnp.einsum('blgp,ghp->blgh', h_im, c_im.astype(f32))
         + d_skip.astype(f32)[None, None] * uf)
    return y.astype(u.dtype)


def even_layer(x, norm_g, w_in, lam_re, lam_im, log_dt, b_re, b_im, c_re, c_im, d_skip,
               w_glu, b_glu, ln_g, ln_b, w_sp, b_sp, w_out):
    bsz, seq, _ = x.shape
    h = rmsnorm(x, norm_g)
    proj = h @ w_in
    xa, ga, zb, gb = jnp.split(proj, [SSM_WIDTH, 2 * SSM_WIDTH, 2 * SSM_WIDTH + 2 * SG_WIDTH], axis=-1)

    ya = s5_mixer(xa.reshape(bsz, seq, SSM_GROUPS, SSM_GROUP), lam_re, lam_im, log_dt,
                  b_re, b_im, c_re, c_im, d_skip.reshape(SSM_GROUPS, SSM_GROUP))
    ya = jax.nn.gelu(ya.reshape(bsz, seq, SSM_WIDTH))
    ya = ya * jax.nn.sigmoid(ya @ w_glu + b_glu)
    ya = ya * jax.nn.silu(ga)

    zb = jax.nn.gelu(zb)
    u, v = jnp.split(zb, 2, axis=-1)
    v = layernorm(v, ln_g, ln_b)
    vc = v.reshape(bsz, seq // SG_CHUNK, SG_CHUNK, SG_HEADS, SG_HEAD_DIM)
    causal = jnp.tril(jnp.ones((SG_CHUNK, SG_CHUNK), dtype=bool))
    w_c = jnp.where(causal[None], w_sp, jnp.zeros((), w_sp.dtype))
    s = jnp.einsum('gts,bnsgc->bntgc', w_c, vc) + b_sp.T[:, :, None]
    yb = u * s.reshape(bsz, seq, SG_WIDTH) * jax.nn.silu(gb)

    y = jnp.concatenate([ya, yb], axis=-1) @ w_out
    return x + y


def partial_rope(t, cos, sin):
    tr = t[..., :ROT_DIM]
    tp = t[..., ROT_DIM:]
    t1, t2 = jnp.split(tr, 2, axis=-1)
    c = cos[None, :, None, None, :]
    s = sin[None, :, None, None, :]
    rot = jnp.concatenate([t1 * c - t2 * s, t2 * c + t1 * s], axis=-1)
    return jnp.concatenate([rot, tp], axis=-1)


def odd_layer(x, norm_g, w_in, lq1, lk1, lq2, lk2, subln_g, w_out, lambda_init):
    bsz, seq, _ = x.shape
    f32 = jnp.float32
    h = rmsnorm(x, norm_g)
    proj = h @ w_in
    q, k, v, g = jnp.split(proj, 4, axis=-1)
    q = q.reshape(bsz, seq, DA_HEADS, 2, DA_HEAD_DIM)
    k = k.reshape(bsz, seq, DA_HEADS, 2, DA_HEAD_DIM)
    v = v.reshape(bsz, seq, DA_HEADS, DA_V_DIM)

    pos = jnp.arange(seq, dtype=f32)
    inv_freq = ROPE_THETA ** (-jnp.arange(0, ROT_DIM, 2, dtype=f32) / ROT_DIM)
    ang = pos[:, None] * inv_freq[None, :]
    cos = jnp.cos(ang).astype(q.dtype)
    sin = jnp.sin(ang).astype(q.dtype)
    q = partial_rope(q, cos, sin)
    k = partial_rope(k, cos, sin)

    lam = (jnp.exp(jnp.sum(lq1.astype(f32) * lk1.astype(f32)))
           - jnp.exp(jnp.sum(lq2.astype(f32) * lk2.astype(f32))) + lambda_init)
    scale = DA_HEAD_DIM ** -0.5
    n_blocks = seq // Q_BLOCK
    qb = q.reshape(bsz, n_blocks, Q_BLOCK, DA_HEADS, 2, DA_HEAD_DIM).transpose(1, 0, 2, 3, 4, 5)
    kpos = jnp.arange(seq)

    def attend_block(args):
        q_blk, blk = args
        sc = jnp.einsum('bqhjd,bkhjd->bhjqk', q_blk, k).astype(f32) * scale
        qpos = blk * Q_BLOCK + jnp.arange(Q_BLOCK)
        mask = kpos[None, :] <= qpos[:, None]
        sc = jnp.where(mask, sc, -jnp.inf)
        p = jax.nn.softmax(sc, axis=-1)
        a = p[:, :, 0] - lam * p[:, :, 1]
        return jnp.einsum('bhqk,bkhe->bqhe', a.astype(v.dtype), v)

    o = lax.map(attend_block, (qb, jnp.arange(n_blocks)))
    o = o.transpose(1, 0, 2, 3, 4).reshape(bsz, seq, DA_HEADS, DA_V_DIM)
    o = rmsnorm(o, subln_g) * (1.0 - lambda_init)
    o = o.reshape(bsz, seq, DA_WIDTH) * jax.nn.silu(g)
    return x + o @ w_out


def setup_inputs(seed: int = 0) -> dict:
    key = jax.random.key(seed)
    keys = list(jax.random.split(key, 32))
    f32 = jnp.float32

    def nrm(shape, std):
        return std * jax.random.normal(keys.pop(), shape, f32)

    ne, no = N_EVEN, N_ODD
    G, P, H = SSM_GROUPS, SSM_STATE, SSM_GROUP
    x = jax.random.normal(keys.pop(), (BATCH, SEQ, D_MODEL), f32)
    ev_norm = 1.0 + nrm((ne, D_MODEL), 0.02)
    ev_w_in = nrm((ne, D_MODEL, EVEN_IN), D_MODEL ** -0.5)
    n_idx = jnp.arange(SSM_STATE, dtype=f32)
    ssm_lam_re = -0.5 + nrm((ne, G, P), 0.01)
    ssm_lam_im = math.pi * n_idx + nrm((ne, G, P), 0.01)
    ssm_log_dt = jax.random.uniform(keys.pop(), (ne, G), f32, math.log(DT_MIN), math.log(DT_MAX))
    ssm_b_re = nrm((ne, G, P, H), (2 * H) ** -0.5)
    ssm_b_im = nrm((ne, G, P, H), (2 * H) ** -0.5)
    ssm_c_re = nrm((ne, G, H, P), P ** -0.5)
    ssm_c_im = nrm((ne, G, H, P), P ** -0.5)
    ssm_d = nrm((ne, SSM_WIDTH), 1.0)
    ssm_w_glu = nrm((ne, SSM_WIDTH, SSM_WIDTH), SSM_WIDTH ** -0.5)
    ssm_b_glu = nrm((ne, SSM_WIDTH), 0.02)
    sg_ln_g = 1.0 + nrm((ne, SG_WIDTH), 0.02)
    sg_ln_b = nrm((ne, SG_WIDTH), 0.02)
    sg_w_sp = nrm((ne, SG_HEADS, SG_CHUNK, SG_CHUNK), SG_CHUNK ** -0.5)
    sg_b_sp = 1.0 + nrm((ne, SG_HEADS, SG_CHUNK), 0.02)
    ev_w_out = nrm((ne, EVEN_MIX, D_MODEL), EVEN_MIX ** -0.5)
    od_norm = 1.0 + nrm((no, D_MODEL), 0.02)
    od_w_in = nrm((no, D_MODEL, ODD_IN), D_MODEL ** -0.5)
    da_lq1 = nrm((no, DA_HEAD_DIM), 0.1)
    da_lk1 = nrm((no, DA_HEAD_DIM), 0.1)
    da_lq2 = nrm((no, DA_HEAD_DIM), 0.1)
    da_lk2 = nrm((no, DA_HEAD_DIM), 0.1)
    da_subln = 1.0 + nrm((no, DA_V_DIM), 0.02)
    od_w_out = nrm((no, DA_WIDTH, D_MODEL), DA_WIDTH ** -0.5)
    final_norm = 1.0 + nrm((D_MODEL,), 0.02)
    return {'x': x, 'ev_norm': ev_norm, 'ev_w_in': ev_w_in,
            'ssm_lam_re': ssm_lam_re, 'ssm_lam_im': ssm_lam_im, 'ssm_log_dt': ssm_log_dt,
            'ssm_b_re': ssm_b_re, 'ssm_b_im': ssm_b_im, 'ssm_c_re': ssm_c_re, 'ssm_c_im': ssm_c_im,
            'ssm_d': ssm_d, 'ssm_w_glu': ssm_w_glu, 'ssm_b_glu': ssm_b_glu,
            'sg_ln_g': sg_ln_g, 'sg_ln_b': sg_ln_b, 'sg_w_sp': sg_w_sp, 'sg_b_sp': sg_b_sp,
            'ev_w_out': ev_w_out, 'od_norm': od_norm, 'od_w_in': od_w_in,
            'da_lq1': da_lq1, 'da_lk1': da_lk1, 'da_lq2': da_lq2, 'da_lk2': da_lk2,
            'da_subln': da_subln, 'od_w_out': od_w_out, 'final_norm': final_norm}


def reference(x, ev_norm, ev_w_in, ssm_lam_re, ssm_lam_im, ssm_log_dt, ssm_b_re, ssm_b_im,
              ssm_c_re, ssm_c_im, ssm_d, ssm_w_glu, ssm_b_glu, sg_ln_g, sg_ln_b, sg_w_sp, sg_b_sp,
              ev_w_out, od_norm, od_w_in, da_lq1, da_lk1, da_lq2, da_lk2, da_subln, od_w_out,
              final_norm):
    for i in range(DEPTH):
        j = i // 2
        if i % 2 == 0:
            x = even_layer(x, ev_norm[j], ev_w_in[j], ssm_lam_re[j], ssm_lam_im[j], ssm_log_dt[j],
                           ssm_b_re[j], ssm_b_im[j], ssm_c_re[j], ssm_c_im[j], ssm_d[j],
                           ssm_w_glu[j], ssm_b_glu[j], sg_ln_g[j], sg_ln_b[j], sg_w_sp[j], sg_b_sp[j],
                           ev_w_out[j])
        else:
            lambda_init = 0.8 - 0.6 * math.exp(-0.3 * i)
            x = odd_layer(x, od_norm[j], od_w_in[j], da_lq1[j], da_lk1[j], da_lq2[j], da_lk2[j],
                          da_subln[j], od_w_out[j], lambda_init)
    return rmsnorm(x, final_norm)
```

```python
import functools
import math

import jax
import jax.numpy as jnp
from jax import lax
from jax.experimental import pallas as pl
from jax.experimental.pallas import tpu as pltpu

F32 = jnp.float32
BF16 = jnp.bfloat16

RMS_EPS = 1e-6
LN_EPS = 1e-5

SSM_GROUP = 16
SSM_STATE = 64
SSM_CHUNK = 16
SG_CHUNK = 128
SG_HEADS = 8
DA_HEADS = 8
DA_HEAD_DIM = 128
DA_V_DIM = 256
ROT_DIM = 32
ROPE_THETA = 500000.0
NEG_BIG = -1e30

VMEM_LIMIT_BYTES = 56 * 1024 * 1024


def _cparams(*sem):
    return pltpu.CompilerParams(dimension_semantics=sem, vmem_limit_bytes=VMEM_LIMIT_BYTES)


def _gelu_tanh(x):
    return 0.5 * x * (1.0 + jnp.tanh(math.sqrt(2.0 / math.pi) * (x + 0.044715 * (x * x * x))))


def _sigmoid(x):
    return 1.0 / (1.0 + jnp.exp(-x))


def _silu(x):
    return x * _sigmoid(x)


def _in_proj_body(x_ref, g_ref, w_ref, *rest, epilogues, n_aux):
    aux = rest[:n_aux]
    outs = rest[n_aux:n_aux + len(epilogues)]
    hn_ref = rest[-1]
    j = pl.program_id(1)

    @pl.when(j == 0)
    def _():
        x = x_ref[...]
        ms = jnp.mean(x * x, axis=-1, keepdims=True)
        hn_ref[...] = ((x * lax.rsqrt(ms + RMS_EPS)) * g_ref[...]).astype(BF16)

    acc = jnp.dot(hn_ref[...], w_ref[...], preferred_element_type=F32)
    for k, ep in enumerate(epilogues):
        @pl.when(j == k)
        def _(k=k, ep=ep):
            outs[k][...] = ep(acc, aux).astype(outs[k].dtype)


def _in_proj(x2, norm_g, w_bf16, epilogues, out_dtypes, aux=(), aux_specs=(), *, tm, tn, name):
    n_tok, d = x2.shape
    n_seg = len(epilogues)
    assert w_bf16.shape == (d, n_seg * tn) and n_tok % tm == 0
    out_shape = [jax.ShapeDtypeStruct((n_tok, tn), dt) for dt in out_dtypes]
    return pl.pallas_call(
        functools.partial(_in_proj_body, epilogues=epilogues, n_aux=len(aux)),
        grid=(n_tok // tm, n_seg),
        in_specs=[pl.BlockSpec((tm, d), lambda i, j: (i, 0)),
                  pl.BlockSpec((1, d), lambda i, j: (0, 0)),
                  pl.BlockSpec((d, tn), lambda i, j: (0, j))] + list(aux_specs),
        out_specs=[pl.BlockSpec((tm, tn), lambda i, j: (i, 0)) for _ in range(n_seg)],
        out_shape=out_shape,
        scratch_shapes=[pltpu.VMEM((tm, d), BF16)],
        compiler_params=_cparams("parallel", "arbitrary"),
        name=name,
    )(x2, norm_g.reshape(1, d), w_bf16, *aux)


def _ep_identity(acc, aux):
    return acc


def _ep_silu(acc, aux):
    return _silu(acc)


def _ep_gelu(acc, aux):
    return _gelu_tanh(acc)


def _ep_gelu_layernorm(acc, aux):
    ln_g, ln_b = aux[0][...], aux[1][...]
    z = _gelu_tanh(acc)
    mu = jnp.mean(z, axis=-1, keepdims=True)
    zc = z - mu
    y = zc * lax.rsqrt(jnp.mean(zc * zc, axis=-1, keepdims=True) + LN_EPS)
    return y * ln_g + ln_b


def _rope(acc, tab):
    c, s_lo, s_hi = tab[:, 0:128], tab[:, 128:256], tab[:, 256:384]
    pieces = []
    for h in range(acc.shape[1] // DA_HEAD_DIM):
        blk = acc[:, h * DA_HEAD_DIM:(h + 1) * DA_HEAD_DIM]
        rot = (blk * c + pltpu.roll(blk, ROT_DIM // 2, 1) * s_lo
               + pltpu.roll(blk, DA_HEAD_DIM - ROT_DIM // 2, 1) * s_hi)
        pieces.append(rot)
    return jnp.concatenate(pieces, axis=1)


def _ep_rope_scaled(acc, aux):
    return _rope(acc, aux[0][...]) * (DA_HEAD_DIM ** -0.5)


def _ep_rope(acc, aux):
    return _rope(acc, aux[0][...])


def _rope_table(seq):
    pos = jnp.arange(seq, dtype=F32)
    inv_freq = ROPE_THETA ** (-jnp.arange(0, ROT_DIM, 2, dtype=F32) / ROT_DIM)
    ang = pos[:, None] * inv_freq[None, :]
    cos, sin = jnp.cos(ang), jnp.sin(ang)
    half = ROT_DIM // 2
    pad = DA_HEAD_DIM - ROT_DIM
    c = jnp.concatenate([cos, cos, jnp.ones((seq, pad), F32)], axis=1)
    s_lo = jnp.concatenate([jnp.zeros((seq, half), F32), sin, jnp.zeros((seq, pad), F32)], axis=1)
    s_hi = jnp.concatenate([-sin, jnp.zeros((seq, half + pad), F32)], axis=1)
    return jnp.concatenate([c, s_lo, s_hi], axis=1)


def _s5_prep_body(lr_c_ref, li_c_ref, ldt_c_ref, lr_r_ref, li_r_ref, ldt_r_ref,
                  bre_c_ref, bim_c_ref, b1_r_ref, b2_r_ref, cre_ref, cim_ref, ctre_ref, ctim_ref,
                  k_ref, wst_ref, wout_ref, dec_ref, *, gb, n_pow):
    t_chunk, h_in, p = SSM_CHUNK, SSM_GROUP, SSM_STATE
    width = t_chunk * h_in

    def discretise(lr, li, ldt):
        dt = jnp.exp(ldt)
        mag = jnp.exp(lr * dt)
        a_re = mag * jnp.cos(li * dt)
        a_im = mag * jnp.sin(li * dt)
        den = lr * lr + li * li
        nr = a_re - 1.0
        f_re = (nr * lr + a_im * li) / den
        f_im = (a_im * lr - nr * li) / den
        return dt, f_re, f_im

    def power(lr, li, dt, n):
        mag = jnp.exp(lr * dt * n)
        return mag * jnp.cos(li * dt * n), mag * jnp.sin(li * dt * n)

    for g in range(gb):
        lr, li = lr_c_ref[g], li_c_ref[g]
        dt, f_re, f_im = discretise(lr, li, ldt_c_ref[g])
        tau = (lax.broadcasted_iota(jnp.int32, (p, width), 1) // h_in).astype(F32)
        e_re, e_im = power(lr, li, dt, tau)
        b_re, b_im = bre_c_ref[g], bim_c_ref[g]
        bb_re = f_re * b_re - f_im * b_im
        bb_im = f_re * b_im + f_im * b_re
        x_re = e_re * bb_re - e_im * bb_im
        x_im = e_re * bb_im + e_im * bb_re
        k_ref[g] = (jnp.dot(cre_ref[g], x_re, preferred_element_type=F32, precision=lax.Precision.HIGHEST)
                    - jnp.dot(cim_ref[g], x_im, preferred_element_type=F32, precision=lax.Precision.HIGHEST))
        e1_re, e1_im = power(lr, li, dt, tau + 1.0)
        ct_re, ct_im = ctre_ref[g], ctim_ref[g]
        wout_ref[g, 0:p, :] = (ct_re * e1_re - ct_im * e1_im).astype(wout_ref.dtype)
        wout_ref[g, p:2 * p, :] = (-ct_re * e1_im - ct_im * e1_re).astype(wout_ref.dtype)

        lr2, li2 = lr_r_ref[g], li_r_ref[g]
        dt2, f2_re, f2_im = discretise(lr2, li2, ldt_r_ref[g])
        lane = lax.broadcasted_iota(jnp.int32, (1, 2 * p), 1)
        sgn = jnp.where(lane < p, -1.0, 1.0).astype(F32)
        b1, b2 = b1_r_ref[g], b2_r_ref[g]
        bb1 = f2_re * b1 + (sgn * f2_im) * b2
        bb2 = f2_re * b2 - (sgn * f2_im) * b1
        s_idx = (lax.broadcasted_iota(jnp.int32, (width, 2 * p), 0) // h_in).astype(F32)
        er, ei = power(lr2, li2, dt2, (t_chunk - 1.0) - s_idx)
        wst_ref[g] = (er * bb1 + (sgn * ei) * bb2).astype(wst_ref.dtype)
        pr, pi_ = power(lr2, li2, dt2, float(t_chunk))
        for k in range(n_pow):
            dec_ref[g, k:k + 1, :] = pr
            dec_ref[g, n_pow + k:n_pow + k + 1, :] = sgn * pi_
            pr, pi_ = pr * pr - pi_ * pi_, 2.0 * pr * pi_


def _s5_prep(lam_re, lam_im, log_dt, b_re, b_im, c_re, c_im, n_pow, *, gb=8):
    g, p = lam_re.shape
    h = SSM_GROUP
    t = SSM_CHUNK
    width = t * h
    col = lambda a: a.reshape(g, p, 1)
    row = lambda a: jnp.concatenate([a, a], axis=1).reshape(g, 1, 2 * p)
    ldt = jnp.broadcast_to(log_dt[:, None], (g, p))
    bre_c = jnp.tile(b_re, (1, 1, t))
    bim_c = jnp.tile(b_im, (1, 1, t))
    brt = jnp.tile(b_re.transpose(0, 2, 1), (1, t, 1))
    bit = jnp.tile(b_im.transpose(0, 2, 1), (1, t, 1))
    b1_r = jnp.concatenate([brt, bit], axis=2)
    b2_r = jnp.concatenate([bit, brt], axis=2)
    ctre = jnp.tile(c_re.transpose(0, 2, 1), (1, 1, t))
    ctim = jnp.tile(c_im.transpose(0, 2, 1), (1, 1, t))
    ins = [col(lam_re), col(lam_im), col(ldt), row(lam_re), row(lam_im), row(ldt),
           bre_c, bim_c, b1_r, b2_r, c_re, c_im, ctre, ctim]
    spec3 = lambda a: pl.BlockSpec((gb,) + a.shape[1:], lambda i: (i, 0, 0))
    out_shape = [jax.ShapeDtypeStruct((g, h, width), F32),
                 jax.ShapeDtypeStruct((g, width, 2 * p), BF16),
                 jax.ShapeDtypeStruct((g, 2 * p, width), BF16),
                 jax.ShapeDtypeStruct((g, 2 * n_pow, 2 * p), F32)]
    return pl.pallas_call(
        functools.partial(_s5_prep_body, gb=gb, n_pow=n_pow),
        grid=(g // gb,),
        in_specs=[spec3(a) for a in ins],
        out_specs=[spec3(s) for s in out_shape],
        out_shape=out_shape,
        compiler_params=_cparams("parallel"),
        name="s5_prep",
    )(*ins)


def _toeplitz(kmat):
    g, h, _ = kmat.shape
    t = SSM_CHUNK
    k4 = kmat.reshape(g, h, t, h)
    s_idx = jnp.arange(t)[:, None]
    t_idx = jnp.arange(t)[None, :]
    lag = t_idx - s_idx
    m = jnp.take(k4, jnp.clip(lag, 0, t - 1).reshape(-1), axis=2)
    m = m.reshape(g, h, t, t, h)
    m = jnp.where((lag >= 0)[None, None, :, :, None], m, 0.0)
    return m.transpose(0, 2, 4, 3, 1).reshape(g, t * h, t * h)


def _s5_body(u_ref, m_ref, wst_ref, wout_ref, dec_ref, d_ref, y_ref, *, n_chunks, n_pow):
    p2 = 2 * SSM_STATE
    u = u_ref[0]
    rows = u.shape[0]
    y = jnp.dot(u, m_ref[0], preferred_element_type=F32)
    st = jnp.dot(u, wst_ref[0], preferred_element_type=F32)
    dec = dec_ref[0]
    cidx = lax.broadcasted_iota(jnp.int32, (rows, p2), 0) % n_chunks
    for k in range(n_pow):
        d = 1 << k
        prev = jnp.where(cidx >= d, pltpu.roll(st, d, 0), 0.0)
        st = st + dec[k:k + 1, :] * prev + dec[n_pow + k:n_pow + k + 1, :] * pltpu.roll(prev, SSM_STATE, 1)
    h_in = jnp.where(cidx >= 1, pltpu.roll(st, 1, 0), 0.0)
    y = y + jnp.dot(h_in.astype(BF16), wout_ref[0], preferred_element_type=F32)
    y_ref[0] = (y + d_ref[0] * u.astype(F32)).astype(y_ref.dtype)


def _s5(u_flat, m, wst, wout, dec, d_flat, n_chunks, n_pow):
    g, rows, width = u_flat.shape
    spec = lambda a: pl.BlockSpec((1,) + a.shape[1:], lambda i: (i, 0, 0))
    return pl.pallas_call(
        functools.partial(_s5_body, n_chunks=n_chunks, n_pow=n_pow),
        grid=(g,),
        in_specs=[spec(u_flat), spec(m), spec(wst), spec(wout), spec(dec), spec(d_flat)],
        out_specs=pl.BlockSpec((1, rows, width), lambda i: (i, 0, 0)),
        out_shape=jax.ShapeDtypeStruct((g, rows, width), F32),
        compiler_params=_cparams("parallel"),
        name="s5_scan",
    )(u_flat, m, wst, wout, dec, d_flat)


def _even_out_body(y_ref, ga_ref, u_ref, v_ref, gb_ref, x_ref, wglu_ref, bglu_ref, wsp_ref, bsp_ref,
                   wout_ref, o_ref, mix_ref, *, tm):
    w = y_ref.shape[1]
    ya = _gelu_tanh(y_ref[...])
    z = jnp.dot(ya.astype(BF16), wglu_ref[...], preferred_element_type=F32) + bglu_ref[...]
    ya = ya * _sigmoid(z)
    ya = ya * ga_ref[...].astype(F32)
    mix_ref[:, 0:w] = ya.astype(BF16)

    tri = (lax.broadcasted_iota(jnp.int32, (SG_CHUNK, SG_CHUNK), 1)
           <= lax.broadcasted_iota(jnp.int32, (SG_CHUNK, SG_CHUNK), 0))
    hd = w // SG_HEADS
    for g in range(SG_HEADS):
        w_c = jnp.where(tri, wsp_ref[g], 0.0).astype(BF16)
        bias = bsp_ref[:, g:g + 1]
        for n in range(tm // SG_CHUNK):
            rs = slice(n * SG_CHUNK, (n + 1) * SG_CHUNK)
            cs = slice(g * hd, (g + 1) * hd)
            s = jnp.dot(w_c, v_ref[rs, cs], preferred_element_type=F32) + bias
            yb = u_ref[rs, cs].astype(F32) * s * gb_ref[rs, cs].astype(F32)
            mix_ref[rs, w + g * hd:w + (g + 1) * hd] = yb.astype(BF16)

    o_ref[...] = x_ref[...] + jnp.dot(mix_ref[...], wout_ref[...], preferred_element_type=F32)


def _even_out(y, ga, u, v, gb, x2, w_glu, b_glu, w_sp, b_sp_t, w_out, *, tm):
    n_tok, d = x2.shape
    w = y.shape[1]
    tok = lambda width: pl.BlockSpec((tm, width), lambda i: (i, 0))
    full = lambda a: pl.BlockSpec(a.shape, lambda i: (0,) * a.ndim)
    return pl.pallas_call(
        functools.partial(_even_out_body, tm=tm),
        grid=(n_tok // tm,),
        in_specs=[tok(w), tok(w), tok(w), tok(w), tok(w), tok(d),
                  full(w_glu), full(b_glu), full(w_sp), full(b_sp_t), full(w_out)],
        out_specs=tok(d),
        out_shape=jax.ShapeDtypeStruct((n_tok, d), F32),
        scratch_shapes=[pltpu.VMEM((tm, 2 * w), BF16)],
        compiler_params=_cparams("parallel"),
        name="even_out",
    )(y, ga, u, v, gb, x2, w_glu, b_glu, w_sp, b_sp_t, w_out)


def _attn_body(lq1_ref, lk1_ref, lq2_ref, lk2_ref, sg_ref, q_ref, k_ref, v_ref, o_ref,
               m1_ref, l1_ref, a1_ref, m2_ref, l2_ref, a2_ref, *, tq, tk, lambda_init):
    qi = pl.program_id(2)
    hd = DA_HEAD_DIM
    lam = (jnp.exp(jnp.sum(lq1_ref[...] * lk1_ref[...])) - jnp.exp(jnp.sum(lq2_ref[...] * lk2_ref[...]))
           + lambda_init)
    q = q_ref[0]
    maps = ((q[:, 0:hd], 0, m1_ref, l1_ref, a1_ref), (q[:, hd:2 * hd], hd, m2_ref, l2_ref, a2_ref))
    for _, _, m_ref, l_ref, a_ref in maps:
        m_ref[...] = jnp.full(m_ref.shape, NEG_BIG, F32)
        l_ref[...] = jnp.zeros(l_ref.shape, F32)
        a_ref[...] = jnp.zeros(a_ref.shape, F32)

    def step(j, masked):
        k0 = pl.multiple_of(j * tk, tk)
        kj = k_ref[0, pl.ds(k0, tk), :]
        vj = v_ref[0, pl.ds(k0, tk), :]
        if masked:
            qpos = qi * tq + lax.broadcasted_iota(jnp.int32, (tq, tk), 0)
            kpos = k0 + lax.broadcasted_iota(jnp.int32, (tq, tk), 1)
            keep = kpos <= qpos
        for qm, off, m_ref, l_ref, a_ref in maps:
            s = lax.dot_general(qm, kj[:, off:off + hd], (((1,), (1,)), ((), ())),
                                preferred_element_type=F32)
            if masked:
                s = jnp.where(keep, s, NEG_BIG)
            m_old = m_ref[...]
            m_new = jnp.maximum(m_old, jnp.max(s, axis=-1, keepdims=True))
            alpha = jnp.exp(m_old - m_new)
            pexp = jnp.exp(s - m_new)
            l_ref[...] = alpha * l_ref[...] + jnp.sum(pexp, axis=-1, keepdims=True)
            a_ref[...] = alpha * a_ref[...] + jnp.dot(pexp.astype(BF16), vj, preferred_element_type=F32)
            m_ref[...] = m_new

    n_full = (qi * tq) // tk

    def full_step(j, c):
        step(j, False)
        return c

    lax.fori_loop(0, n_full, full_step, 0)
    for d in range(tq // tk):
        step(n_full + d, True)

    o = a1_ref[...] / l1_ref[...] - lam * (a2_ref[...] / l2_ref[...])
    ms = jnp.mean(o * o, axis=-1, keepdims=True)
    o = (o * lax.rsqrt(ms + RMS_EPS)) * sg_ref[...]
    o_ref[0] = (o * (1.0 - lambda_init)).astype(o_ref.dtype)


def _diff_attn(q, k, v, lq1, lk1, lq2, lk2, subln_g, lambda_init, *, tq, tk):
    b, seq, width = q.shape
    vec = lambda a: a.reshape(1, -1)
    small = lambda n: pl.BlockSpec((1, n), lambda bi, hi, qi: (0, 0))
    scratch = []
    for _ in range(2):
        scratch += [pltpu.VMEM((tq, 1), F32), pltpu.VMEM((tq, 1), F32), pltpu.VMEM((tq, DA_V_DIM), F32)]
    return pl.pallas_call(
        functools.partial(_attn_body, tq=tq, tk=tk, lambda_init=lambda_init),
        grid=(b, DA_HEADS, seq // tq),
        in_specs=[small(DA_HEAD_DIM)] * 4 + [small(DA_V_DIM)] + [
            pl.BlockSpec((1, tq, DA_V_DIM), lambda bi, hi, qi: (bi, qi, hi)),
            pl.BlockSpec((1, seq, DA_V_DIM), lambda bi, hi, qi: (bi, 0, hi)),
            pl.BlockSpec((1, seq, DA_V_DIM), lambda bi, hi, qi: (bi, 0, hi))],
        out_specs=pl.BlockSpec((1, tq, DA_V_DIM), lambda bi, hi, qi: (bi, qi, hi)),
        out_shape=jax.ShapeDtypeStruct((b, seq, width), BF16),
        scratch_shapes=scratch,
        compiler_params=_cparams("parallel", "parallel", "arbitrary"),
        name="diff_attn",
    )(vec(lq1), vec(lk1), vec(lq2), vec(lk2), vec(subln_g), q, k, v)


def _odd_out_body(o_ref, g_ref, x_ref, w_ref, fn_ref, out_ref, *, final_norm):
    gated = (o_ref[...].astype(F32) * g_ref[...].astype(F32)).astype(BF16)
    y = x_ref[...] + jnp.dot(gated, w_ref[...], preferred_element_type=F32)
    if final_norm:
        ms = jnp.mean(y * y, axis=-1, keepdims=True)
        y = (y * lax.rsqrt(ms + RMS_EPS)) * fn_ref[...]
    out_ref[...] = y


def _odd_out(o, g, x2, w_out, final_g, *, tm, final_norm):
    n_tok, d = x2.shape
    tok = lambda dt: pl.BlockSpec((tm, d), lambda i: (i, 0))
    return pl.pallas_call(
        functools.partial(_odd_out_body, final_norm=final_norm),
        grid=(n_tok // tm,),
        in_specs=[tok(BF16), tok(BF16), tok(F32),
                  pl.BlockSpec(w_out.shape, lambda i: (0, 0)),
                  pl.BlockSpec((1, d), lambda i: (0, 0))],
        out_specs=tok(F32),
        out_shape=jax.ShapeDtypeStruct((n_tok, d), F32),
        compiler_params=_cparams("parallel"),
        name="odd_out",
    )(o, g, x2, w_out, final_g.reshape(1, d))


def _even_layer(x2, bsz, seq, norm_g, w_in, lam_re, lam_im, log_dt, b_re, b_im, c_re, c_im, d_skip,
                w_glu, b_glu, ln_g, ln_b, w_sp, b_sp, w_out):
    n_tok, d = x2.shape
    w = d // 2
    groups = w // SSM_GROUP
    t = SSM_CHUNK
    n_chunks = seq // t
    n_pow = max(1, (n_chunks - 1).bit_length())

    full = lambda a: pl.BlockSpec(a.shape, lambda i, j: (0, 0))
    ln_g2, ln_b2 = ln_g.reshape(1, w), ln_b.reshape(1, w)
    xa, ga, u, v, gb = _in_proj(
        x2, norm_g, w_in.astype(BF16),
        [_ep_identity, _ep_silu, _ep_gelu, _ep_gelu_layernorm, _ep_silu],
        [BF16] * 5, aux=(ln_g2, ln_b2), aux_specs=(full(ln_g2), full(ln_b2)),
        tm=512, tn=w, name="even_in_proj")

    kmat, wst, wout_s, dec = _s5_prep(lam_re, lam_im, log_dt, b_re, b_im, c_re, c_im, n_pow)
    m = _toeplitz(kmat).astype(BF16)
    u_flat = (xa.reshape(bsz, n_chunks, t, groups, SSM_GROUP).transpose(3, 0, 1, 2, 4)
              .reshape(groups, bsz * n_chunks, t * SSM_GROUP))
    d_flat = jnp.tile(d_skip.reshape(groups, 1, SSM_GROUP), (1, 1, t))
    y_flat = _s5(u_flat, m, wst, wout_s, dec, d_flat, n_chunks, n_pow)
    y = (y_flat.reshape(groups, bsz, n_chunks, t, SSM_GROUP).transpose(1, 2, 3, 0, 4)
         .reshape(n_tok, w))

    return _even_out(y, ga, u, v, gb, x2, w_glu.astype(BF16), b_glu.reshape(1, w),
                     w_sp, b_sp.T, w_out.astype(BF16), tm=512)


def _odd_layer(x2, bsz, seq, norm_g, w_in, lq1, lk1, lq2, lk2, subln_g, w_out, lambda_init,
               final_g, final_norm):
    n_tok, d = x2.shape
    tm = 512
    tab = _rope_table(seq)
    n_pos_blocks = seq // tm
    tab_spec = pl.BlockSpec((tm, tab.shape[1]), lambda i, j: (i % n_pos_blocks, 0))
    q, k, v, g = _in_proj(
        x2, norm_g, w_in.astype(BF16),
        [_ep_rope_scaled, _ep_rope, _ep_identity, _ep_silu],
        [BF16] * 4, aux=(tab,), aux_specs=(tab_spec,), tm=tm, tn=d, name="odd_in_proj")
    shp = (bsz, seq, d)
    o = _diff_attn(q.reshape(shp), k.reshape(shp), v.reshape(shp), lq1, lk1, lq2, lk2, subln_g,
                   lambda_init, tq=256, tk=256)
    return _odd_out(o.reshape(n_tok, d), g, x2, w_out.astype(BF16), final_g, tm=512,
                    final_norm=final_norm)


def kernel(x, ev_norm, ev_w_in, ssm_lam_re, ssm_lam_im, ssm_log_dt, ssm_b_re, ssm_b_im, ssm_c_re, ssm_c_im, ssm_d, ssm_w_glu, ssm_b_glu, sg_ln_g, sg_ln_b, sg_w_sp, sg_b_sp, ev_w_out, od_norm, od_w_in, da_lq1, da_lk1, da_lq2, da_lk2, da_subln, od_w_out, final_norm):
    bsz, seq, d = x.shape
    depth = ev_norm.shape[0] + od_norm.shape[0]
    x2 = x.reshape(bsz * seq, d)
    for i in range(depth):
        j = i // 2
        if i % 2 == 0:
            x2 = _even_layer(x2, bsz, seq, ev_norm[j], ev_w_in[j], ssm_lam_re[j], ssm_lam_im[j],
                             ssm_log_dt[j], ssm_b_re[j], ssm_b_im[j], ssm_c_re[j], ssm_c_im[j],
                             ssm_d[j], ssm_w_glu[j], ssm_b_glu[j], sg_ln_g[j], sg_ln_b[j],
                             sg_w_sp[j], sg_b_sp[j], ev_w_out[j])
        else:
            lambda_init = 0.8 - 0.6 * math.exp(-0.3 * i)
            x2 = _odd_layer(x2, bsz, seq, od_norm[j], od_w_in[j], da_lq1[j], da_lk1[j], da_lq2[j],
                            da_lk2[j], da_subln[j], od_w_out[j], lambda_init, final_norm,
                            final_norm=(i == depth - 1))
    return x2.reshape(bsz, seq, d)
```

```python
import functools
import math

import jax
import jax.numpy as jnp
from jax import lax
from jax.experimental import pallas as pl
from jax.experimental.pallas import tpu as pltpu

F32 = jnp.float32
BF16 = jnp.bfloat16

RMS_EPS = 1e-6
LN_EPS = 1e-5

SSM_GROUP = 16
SSM_STATE = 64
SSM_CHUNK = 16
SG_CHUNK = 128
SG_HEADS = 8
DA_HEADS = 8
DA_HEAD_DIM = 128
DA_V_DIM = 256
ROT_DIM = 32
ROPE_THETA = 500000.0
NEG_BIG = -1e30
ATTN_TILE = 512

VMEM_LIMIT_BYTES = 56 * 1024 * 1024


def _cparams(*sem):
    return pltpu.CompilerParams(dimension_semantics=sem, vmem_limit_bytes=VMEM_LIMIT_BYTES)


def _gelu_tanh(x):
    return 0.5 * x * (1.0 + jnp.tanh(math.sqrt(2.0 / math.pi) * (x + 0.044715 * (x * x * x))))


def _sigmoid(x):
    return 1.0 / (1.0 + jnp.exp(-x))


def _silu(x):
    return x * _sigmoid(x)


def _in_proj_body(x_ref, g_ref, w_ref, *rest, epilogues, n_aux):
    aux = rest[:n_aux]
    outs = rest[n_aux:n_aux + len(epilogues)]
    hn_ref = rest[-1]
    j = pl.program_id(1)

    @pl.when(j == 0)
    def _():
        x = x_ref[...]
        ms = jnp.mean(x * x, axis=-1, keepdims=True)
        hn_ref[...] = ((x * lax.rsqrt(ms + RMS_EPS)) * g_ref[...]).astype(BF16)

    acc = jnp.dot(hn_ref[...], w_ref[...], preferred_element_type=F32)
    for k, ep in enumerate(epilogues):
        @pl.when(j == k)
        def _(k=k, ep=ep):
            outs[k][...] = ep(acc, aux).astype(outs[k].dtype)


def _in_proj(x2, norm_g, w_bf16, epilogues, out_dtypes, aux=(), aux_specs=(), *, tm, tn, name):
    n_tok, d = x2.shape
    n_seg = len(epilogues)
    assert w_bf16.shape == (d, n_seg * tn) and n_tok % tm == 0
    out_shape = [jax.ShapeDtypeStruct((n_tok, tn), dt) for dt in out_dtypes]
    return pl.pallas_call(
        functools.partial(_in_proj_body, epilogues=epilogues, n_aux=len(aux)),
        grid=(n_tok // tm, n_seg),
        in_specs=[pl.BlockSpec((tm, d), lambda i, j: (i, 0)),
                  pl.BlockSpec((1, d), lambda i, j: (0, 0)),
                  pl.BlockSpec((d, tn), lambda i, j: (0, j))] + list(aux_specs),
        out_specs=[pl.BlockSpec((tm, tn), lambda i, j: (i, 0)) for _ in range(n_seg)],
        out_shape=out_shape,
        scratch_shapes=[pltpu.VMEM((tm, d), BF16)],
        compiler_params=_cparams("parallel", "arbitrary"),
        name=name,
    )(x2, norm_g.reshape(1, d), w_bf16, *aux)


def _ep_identity(acc, aux):
    return acc


def _ep_silu(acc, aux):
    return _silu(acc)


def _ep_gelu(acc, aux):
    return _gelu_tanh(acc)


def _ep_gelu_layernorm(acc, aux):
    ln_g, ln_b = aux[0][...], aux[1][...]
    z = _gelu_tanh(acc)
    mu = jnp.mean(z, axis=-1, keepdims=True)
    zc = z - mu
    y = zc * lax.rsqrt(jnp.mean(zc * zc, axis=-1, keepdims=True) + LN_EPS)
    return y * ln_g + ln_b


def _rope(acc, tab):
    c, s_lo, s_hi = tab[:, 0:128], tab[:, 128:256], tab[:, 256:384]
    pieces = []
    for h in range(acc.shape[1] // DA_HEAD_DIM):
        blk = acc[:, h * DA_HEAD_DIM:(h + 1) * DA_HEAD_DIM]
        rot = (blk * c + pltpu.roll(blk, ROT_DIM // 2, 1) * s_lo
               + pltpu.roll(blk, DA_HEAD_DIM - ROT_DIM // 2, 1) * s_hi)
        pieces.append(rot)
    return jnp.concatenate(pieces, axis=1)


def _ep_rope_scaled(acc, aux):
    return _rope(acc, aux[0][...]) * (DA_HEAD_DIM ** -0.5 * math.log2(math.e))


def _ep_rope(acc, aux):
    return _rope(acc, aux[0][...])


def _rope_table(seq):
    pos = jnp.arange(seq, dtype=F32)
    inv_freq = ROPE_THETA ** (-jnp.arange(0, ROT_DIM, 2, dtype=F32) / ROT_DIM)
    ang = pos[:, None] * inv_freq[None, :]
    cos, sin = jnp.cos(ang), jnp.sin(ang)
    half = ROT_DIM // 2
    pad = DA_HEAD_DIM - ROT_DIM
    c = jnp.concatenate([cos, cos, jnp.ones((seq, pad), F32)], axis=1)
    s_lo = jnp.concatenate([jnp.zeros((seq, half), F32), sin, jnp.zeros((seq, pad), F32)], axis=1)
    s_hi = jnp.concatenate([-sin, jnp.zeros((seq, half + pad), F32)], axis=1)
    return jnp.concatenate([c, s_lo, s_hi], axis=1)


def _s5_prep_body(lr_c_ref, li_c_ref, ldt_c_ref, lr_r_ref, li_r_ref, ldt_r_ref,
                  bre_c_ref, bim_c_ref, b1_r_ref, b2_r_ref, cre_ref, cim_ref, ctre_ref, ctim_ref,
                  k_ref, wst_ref, wout_ref, dec_ref, *, gb, n_pow):
    t_chunk, h_in, p = SSM_CHUNK, SSM_GROUP, SSM_STATE
    width = t_chunk * h_in

    def discretise(lr, li, ldt):
        dt = jnp.exp(ldt)
        mag = jnp.exp(lr * dt)
        a_re = mag * jnp.cos(li * dt)
        a_im = mag * jnp.sin(li * dt)
        den = lr * lr + li * li
        nr = a_re - 1.0
        f_re = (nr * lr + a_im * li) / den
        f_im = (a_im * lr - nr * li) / den
        return dt, f_re, f_im

    def power(lr, li, dt, n):
        mag = jnp.exp(lr * dt * n)
        return mag * jnp.cos(li * dt * n), mag * jnp.sin(li * dt * n)

    for g in range(gb):
        lr, li = lr_c_ref[g], li_c_ref[g]
        dt, f_re, f_im = discretise(lr, li, ldt_c_ref[g])
        tau = (lax.broadcasted_iota(jnp.int32, (p, width), 1) // h_in).astype(F32)
        e_re, e_im = power(lr, li, dt, tau)
        b_re, b_im = bre_c_ref[g], bim_c_ref[g]
        bb_re = f_re * b_re - f_im * b_im
        bb_im = f_re * b_im + f_im * b_re
        x_re = e_re * bb_re - e_im * bb_im
        x_im = e_re * bb_im + e_im * bb_re
        k_ref[g] = (jnp.dot(cre_ref[g], x_re, preferred_element_type=F32, precision=lax.Precision.HIGHEST)
                    - jnp.dot(cim_ref[g], x_im, preferred_element_type=F32, precision=lax.Precision.HIGHEST))
        e1_re, e1_im = power(lr, li, dt, tau + 1.0)
        ct_re, ct_im = ctre_ref[g], ctim_ref[g]
        wout_ref[g, 0:p, :] = (ct_re * e1_re - ct_im * e1_im).astype(wout_ref.dtype)
        wout_ref[g, p:2 * p, :] = (-ct_re * e1_im - ct_im * e1_re).astype(wout_ref.dtype)

        lr2, li2 = lr_r_ref[g], li_r_ref[g]
        dt2, f2_re, f2_im = discretise(lr2, li2, ldt_r_ref[g])
        lane = lax.broadcasted_iota(jnp.int32, (1, 2 * p), 1)
        sgn = jnp.where(lane < p, -1.0, 1.0).astype(F32)
        b1, b2 = b1_r_ref[g], b2_r_ref[g]
        bb1 = f2_re * b1 + (sgn * f2_im) * b2
        bb2 = f2_re * b2 - (sgn * f2_im) * b1
        s_idx = (lax.broadcasted_iota(jnp.int32, (width, 2 * p), 0) // h_in).astype(F32)
        er, ei = power(lr2, li2, dt2, (t_chunk - 1.0) - s_idx)
        wst_ref[g] = (er * bb1 + (sgn * ei) * bb2).astype(wst_ref.dtype)
        pr, pi_ = power(lr2, li2, dt2, float(t_chunk))
        for k in range(n_pow):
            dec_ref[g, k:k + 1, :] = pr
            dec_ref[g, n_pow + k:n_pow + k + 1, :] = sgn * pi_
            pr, pi_ = pr * pr - pi_ * pi_, 2.0 * pr * pi_


def _s5_prep(lam_re, lam_im, log_dt, b_re, b_im, c_re, c_im, n_pow, *, gb=8):
    g, p = lam_re.shape
    h = SSM_GROUP
    t = SSM_CHUNK
    width = t * h
    col = lambda a: a.reshape(g, p, 1)
    row = lambda a: jnp.concatenate([a, a], axis=1).reshape(g, 1, 2 * p)
    ldt = jnp.broadcast_to(log_dt[:, None], (g, p))
    bre_c = jnp.tile(b_re, (1, 1, t))
    bim_c = jnp.tile(b_im, (1, 1, t))
    brt = jnp.tile(b_re.transpose(0, 2, 1), (1, t, 1))
    bit = jnp.tile(b_im.transpose(0, 2, 1), (1, t, 1))
    b1_r = jnp.concatenate([brt, bit], axis=2)
    b2_r = jnp.concatenate([bit, brt], axis=2)
    ctre = jnp.tile(c_re.transpose(0, 2, 1), (1, 1, t))
    ctim = jnp.tile(c_im.transpose(0, 2, 1), (1, 1, t))
    ins = [col(lam_re), col(lam_im), col(ldt), row(lam_re), row(lam_im), row(ldt),
           bre_c, bim_c, b1_r, b2_r, c_re, c_im, ctre, ctim]
    spec3 = lambda a: pl.BlockSpec((gb,) + a.shape[1:], lambda i: (i, 0, 0))
    out_shape = [jax.ShapeDtypeStruct((g, h, width), F32),
                 jax.ShapeDtypeStruct((g, width, 2 * p), BF16),
                 jax.ShapeDtypeStruct((g, 2 * p, width), BF16),
                 jax.ShapeDtypeStruct((g, 2 * n_pow, 2 * p), F32)]
    return pl.pallas_call(
        functools.partial(_s5_prep_body, gb=gb, n_pow=n_pow),
        grid=(g // gb,),
        in_specs=[spec3(a) for a in ins],
        out_specs=[spec3(s) for s in out_shape],
        out_shape=out_shape,
        compiler_params=_cparams("parallel"),
        name="s5_prep",
    )(*ins)


def _toeplitz(kmat):
    g, h, _ = kmat.shape
    t = SSM_CHUNK
    k4 = kmat.reshape(g, h, t, h)
    s_idx = jnp.arange(t)[:, None]
    t_idx = jnp.arange(t)[None, :]
    lag = t_idx - s_idx
    m = jnp.take(k4, jnp.clip(lag, 0, t - 1).reshape(-1), axis=2)
    m = m.reshape(g, h, t, t, h)
    m = jnp.where((lag >= 0)[None, None, :, :, None], m, 0.0)
    return m.transpose(0, 2, 4, 3, 1).reshape(g, t * h, t * h)


def _s5_body(u_ref, m_ref, wst_ref, wout_ref, dec_ref, d_ref, y_ref, *, n_chunks, n_pow):
    p2 = 2 * SSM_STATE
    u = u_ref[0]
    rows = u.shape[0]
    y = jnp.dot(u, m_ref[0], preferred_element_type=F32)
    st = jnp.dot(u, wst_ref[0], preferred_element_type=F32)
    dec = dec_ref[0]
    cidx = lax.broadcasted_iota(jnp.int32, (rows, p2), 0) % n_chunks
    for k in range(n_pow):
        d = 1 << k
        prev = jnp.where(cidx >= d, pltpu.roll(st, d, 0), 0.0)
        st = st + dec[k:k + 1, :] * prev + dec[n_pow + k:n_pow + k + 1, :] * pltpu.roll(prev, SSM_STATE, 1)
    h_in = jnp.where(cidx >= 1, pltpu.roll(st, 1, 0), 0.0)
    y = y + jnp.dot(h_in.astype(BF16), wout_ref[0], preferred_element_type=F32)
    y_ref[0] = (y + d_ref[0] * u.astype(F32)).astype(y_ref.dtype)


def _s5(u_flat, m, wst, wout, dec, d_flat, n_chunks, n_pow):
    g, rows, width = u_flat.shape
    spec = lambda a: pl.BlockSpec((1,) + a.shape[1:], lambda i: (i, 0, 0))
    return pl.pallas_call(
        functools.partial(_s5_body, n_chunks=n_chunks, n_pow=n_pow),
        grid=(g,),
        in_specs=[spec(u_flat), spec(m), spec(wst), spec(wout), spec(dec), spec(d_flat)],
        out_specs=pl.BlockSpec((1, rows, width), lambda i: (i, 0, 0)),
        out_shape=jax.ShapeDtypeStruct((g, rows, width), F32),
        compiler_params=_cparams("parallel"),
        name="s5_scan",
    )(u_flat, m, wst, wout, dec, d_flat)


def _even_out_body(y_ref, ga_ref, u_ref, v_ref, gb_ref, x_ref, wglu_ref, bglu_ref, wsp_ref, bsp_ref,
                   wout_ref, o_ref, mix_ref, *, tm):
    w = y_ref.shape[1]
    ya = _gelu_tanh(y_ref[...])
    z = jnp.dot(ya.astype(BF16), wglu_ref[...], preferred_element_type=F32) + bglu_ref[...]
    ya = ya * _sigmoid(z)
    ya = ya * ga_ref[...].astype(F32)
    mix_ref[:, 0:w] = ya.astype(BF16)

    tri = (lax.broadcasted_iota(jnp.int32, (SG_CHUNK, SG_CHUNK), 1)
           <= lax.broadcasted_iota(jnp.int32, (SG_CHUNK, SG_CHUNK), 0))
    hd = w // SG_HEADS
    for g in range(SG_HEADS):
        w_c = jnp.where(tri, wsp_ref[g], 0.0).astype(BF16)
        bias = bsp_ref[:, g:g + 1]
        for n in range(tm // SG_CHUNK):
            rs = slice(n * SG_CHUNK, (n + 1) * SG_CHUNK)
            cs = slice(g * hd, (g + 1) * hd)
            s = jnp.dot(w_c, v_ref[rs, cs], preferred_element_type=F32) + bias
            yb = u_ref[rs, cs].astype(F32) * s * gb_ref[rs, cs].astype(F32)
            mix_ref[rs, w + g * hd:w + (g + 1) * hd] = yb.astype(BF16)

    o_ref[...] = x_ref[...] + jnp.dot(mix_ref[...], wout_ref[...], preferred_element_type=F32)


def _even_out(y, ga, u, v, gb, x2, w_glu, b_glu, w_sp, b_sp_t, w_out, *, tm):
    n_tok, d = x2.shape
    w = y.shape[1]
    tok = lambda width: pl.BlockSpec((tm, width), lambda i: (i, 0))
    full = lambda a: pl.BlockSpec(a.shape, lambda i: (0,) * a.ndim)
    return pl.pallas_call(
        functools.partial(_even_out_body, tm=tm),
        grid=(n_tok // tm,),
        in_specs=[tok(w), tok(w), tok(w), tok(w), tok(w), tok(d),
                  full(w_glu), full(b_glu), full(w_sp), full(b_sp_t), full(w_out)],
        out_specs=tok(d),
        out_shape=jax.ShapeDtypeStruct((n_tok, d), F32),
        scratch_shapes=[pltpu.VMEM((tm, 2 * w), BF16)],
        compiler_params=_cparams("parallel"),
        name="even_out",
    )(y, ga, u, v, gb, x2, w_glu, b_glu, w_sp, b_sp_t, w_out)


def _attn_body(lq1_ref, lk1_ref, lq2_ref, lk2_ref, sg_ref, q_ref, k_ref, v_ref, o_ref,
               s_ref, mx_ref, mrep_ref, ls_ref, acc_ref, *, tq, lambda_init):
    qi = pl.program_id(2)
    hd = DA_HEAD_DIM
    n_lane = tq // 128
    lam = (jnp.exp(jnp.sum(lq1_ref[...] * lk1_ref[...])) - jnp.exp(jnp.sum(lq2_ref[...] * lk2_ref[...]))
           + lambda_init)
    q = q_ref[0]
    qs = (q[:, 0:hd], q[:, hd:2 * hd])
    mx_ref[...] = jnp.full(mx_ref.shape, NEG_BIG, F32)
    ls_ref[...] = jnp.zeros(ls_ref.shape, F32)
    acc_ref[...] = jnp.zeros(acc_ref.shape, F32)

    def lane_fold(x, op):
        out = x[:, 0:128]
        for c in range(1, n_lane):
            out = op(out, x[:, c * 128:(c + 1) * 128])
        return out

    def scores(j, masked):
        k0 = pl.multiple_of(j * tq, tq)
        kj = k_ref[0, pl.ds(k0, tq), :]
        for m in range(2):
            s = lax.dot_general(qs[m], kj[:, m * hd:(m + 1) * hd], (((1,), (1,)), ((), ())),
                                preferred_element_type=F32)
            if masked:
                keep = (lax.broadcasted_iota(jnp.int32, (tq, tq), 1)
                        <= lax.broadcasted_iota(jnp.int32, (tq, tq), 0))
                s = jnp.where(keep, s, NEG_BIG)
            s_ref[m, j] = s
            mx_ref[m] = jnp.maximum(mx_ref[m], lane_fold(s, jnp.maximum))

    def scores_step(j, c):
        scores(j, False)
        return c

    lax.fori_loop(0, qi, scores_step, 0)
    scores(qi, True)

    for m in range(2):
        mrep_ref[m] = jnp.broadcast_to(jnp.max(mx_ref[m], axis=-1, keepdims=True), (tq, 128))

    def accumulate(j, c):
        k0 = pl.multiple_of(j * tq, tq)
        vj = v_ref[0, pl.ds(k0, tq), :]
        for m in range(2):
            s = s_ref[m, j]
            mrep = mrep_ref[m]
            p = jnp.concatenate([jnp.exp2(s[:, c_ * 128:(c_ + 1) * 128] - mrep) for c_ in range(n_lane)],
                                axis=1)
            ls_ref[m] += lane_fold(p, jnp.add)
            acc_ref[m] += jnp.dot(p.astype(BF16), vj, preferred_element_type=F32)
        return c

    lax.fori_loop(0, qi + 1, accumulate, 0)

    o1 = acc_ref[0] / jnp.sum(ls_ref[0], axis=-1, keepdims=True)
    o2 = acc_ref[1] / jnp.sum(ls_ref[1], axis=-1, keepdims=True)
    o = o1 - lam * o2
    ms = jnp.mean(o * o, axis=-1, keepdims=True)
    o = (o * lax.rsqrt(ms + RMS_EPS)) * sg_ref[...]
    o_ref[0] = (o * (1.0 - lambda_init)).astype(o_ref.dtype)


def _diff_attn(q, k, v, lq1, lk1, lq2, lk2, subln_g, lambda_init, *, tq):
    b, seq, width = q.shape
    vec = lambda a: a.reshape(1, -1)
    small = lambda n: pl.BlockSpec((1, n), lambda bi, hi, qi: (0, 0))
    scratch = [pltpu.VMEM((2, seq // tq, tq, tq), F32),
               pltpu.VMEM((2, tq, 128), F32),
               pltpu.VMEM((2, tq, 128), F32),
               pltpu.VMEM((2, tq, 128), F32),
               pltpu.VMEM((2, tq, DA_V_DIM), F32)]
    return pl.pallas_call(
        functools.partial(_attn_body, tq=tq, lambda_init=lambda_init),
        grid=(b, DA_HEADS, seq // tq),
        in_specs=[small(DA_HEAD_DIM)] * 4 + [small(DA_V_DIM)] + [
            pl.BlockSpec((1, tq, DA_V_DIM), lambda bi, hi, qi: (bi, qi, hi)),
            pl.BlockSpec((1, seq, DA_V_DIM), lambda bi, hi, qi: (bi, 0, hi)),
            pl.BlockSpec((1, seq, DA_V_DIM), lambda bi, hi, qi: (bi, 0, hi))],
        out_specs=pl.BlockSpec((1, tq, DA_V_DIM), lambda bi, hi, qi: (bi, qi, hi)),
        out_shape=jax.ShapeDtypeStruct((b, seq, width), BF16),
        scratch_shapes=scratch,
        compiler_params=_cparams("parallel", "parallel", "arbitrary"),
        name="diff_attn",
    )(vec(lq1), vec(lk1), vec(lq2), vec(lk2), vec(subln_g), q, k, v)


def _odd_out_body(o_ref, g_ref, x_ref, w_ref, fn_ref, out_ref, *, final_norm):
    gated = (o_ref[...].astype(F32) * g_ref[...].astype(F32)).astype(BF16)
    y = x_ref[...] + jnp.dot(gated, w_ref[...], preferred_element_type=F32)
    if final_norm:
        ms = jnp.mean(y * y, axis=-1, keepdims=True)
        y = (y * lax.rsqrt(ms + RMS_EPS)) * fn_ref[...]
    out_ref[...] = y


def _odd_out(o, g, x2, w_out, final_g, *, tm, final_norm):
    n_tok, d = x2.shape
    tok = lambda dt: pl.BlockSpec((tm, d), lambda i: (i, 0))
    return pl.pallas_call(
        functools.partial(_odd_out_body, final_norm=final_norm),
        grid=(n_tok // tm,),
        in_specs=[tok(BF16), tok(BF16), tok(F32),
                  pl.BlockSpec(w_out.shape, lambda i: (0, 0)),
                  pl.BlockSpec((1, d), lambda i: (0, 0))],
        out_specs=tok(F32),
        out_shape=jax.ShapeDtypeStruct((n_tok, d), F32),
        compiler_params=_cparams("parallel"),
        name="odd_out",
    )(o, g, x2, w_out, final_g.reshape(1, d))


def _even_layer(x2, bsz, seq, norm_g, w_in, lam_re, lam_im, log_dt, b_re, b_im, c_re, c_im, d_skip,
                w_glu, b_glu, ln_g, ln_b, w_sp, b_sp, w_out):
    n_tok, d = x2.shape
    w = d // 2
    groups = w // SSM_GROUP
    t = SSM_CHUNK
    n_chunks = seq // t
    n_pow = max(1, (n_chunks - 1).bit_length())

    full = lambda a: pl.BlockSpec(a.shape, lambda i, j: (0, 0))
    ln_g2, ln_b2 = ln_g.reshape(1, w), ln_b.reshape(1, w)
    xa, ga, u, v, gb = _in_proj(
        x2, norm_g, w_in.astype(BF16),
        [_ep_identity, _ep_silu, _ep_gelu, _ep_gelu_layernorm, _ep_silu],
        [BF16] * 5, aux=(ln_g2, ln_b2), aux_specs=(full(ln_g2), full(ln_b2)),
        tm=512, tn=w, name="even_in_proj")

    kmat, wst, wout_s, dec = _s5_prep(lam_re, lam_im, log_dt, b_re, b_im, c_re, c_im, n_pow)
    m = _toeplitz(kmat).astype(BF16)
    u_flat = (xa.reshape(bsz, n_chunks, t, groups, SSM_GROUP).transpose(3, 0, 1, 2, 4)
              .reshape(groups, bsz * n_chunks, t * SSM_GROUP))
    d_flat = jnp.tile(d_skip.reshape(groups, 1, SSM_GROUP), (1, 1, t))
    y_flat = _s5(u_flat, m, wst, wout_s, dec, d_flat, n_chunks, n_pow)
    y = (y_flat.reshape(groups, bsz, n_chunks, t, SSM_GROUP).transpose(1, 2, 3, 0, 4)
         .reshape(n_tok, w))

    return _even_out(y, ga, u, v, gb, x2, w_glu.astype(BF16), b_glu.reshape(1, w),
                     w_sp, b_sp.T, w_out.astype(BF16), tm=512)


def _odd_layer(x2, bsz, seq, norm_g, w_in, lq1, lk1, lq2, lk2, subln_g, w_out, lambda_init,
               final_g, final_norm):
    n_tok, d = x2.shape
    tm = 512
    tab = _rope_table(seq)
    n_pos_blocks = seq // tm
    tab_spec = pl.BlockSpec((tm, tab.shape[1]), lambda i, j: (i % n_pos_blocks, 0))
    q, k, v, g = _in_proj(
        x2, norm_g, w_in.astype(BF16),
        [_ep_rope_scaled, _ep_rope, _ep_identity, _ep_silu],
        [BF16] * 4, aux=(tab,), aux_specs=(tab_spec,), tm=tm, tn=d, name="odd_in_proj")
    shp = (bsz, seq, d)
    o = _diff_attn(q.reshape(shp), k.reshape(shp), v.reshape(shp), lq1, lk1, lq2, lk2, subln_g,
                   lambda_init, tq=ATTN_TILE)
    return _odd_out(o.reshape(n_tok, d), g, x2, w_out.astype(BF16), final_g, tm=512,
                    final_norm=final_norm)


def kernel(x, ev_norm, ev_w_in, ssm_lam_re, ssm_lam_im, ssm_log_dt, ssm_b_re, ssm_b_im, ssm_c_re, ssm_c_im, ssm_d, ssm_w_glu, ssm_b_glu, sg_ln_g, sg_ln_b, sg_w_sp, sg_b_sp, ev_w_out, od_norm, od_w_in, da_lq1, da_lk1, da_lq2, da_lk2, da_subln, od_w_out, final_norm):
    bsz, seq, d = x.shape
    depth = ev_norm.shape[0] + od_norm.shape[0]
    x2 = x.reshape(bsz * seq, d)
    for i in range(depth):
        j = i // 2
        if i % 2 == 0:
            x2 = _even_layer(x2, bsz, seq, ev_norm[j], ev_w_in[j], ssm_lam_re[j], ssm_lam_im[j],
                             ssm_log_dt[j], ssm_b_re[j], ssm_b_im[j], ssm_c_re[j], ssm_c_im[j],
                             ssm_d[j], ssm_w_glu[j], ssm_b_glu[j], sg_ln_g[j], sg_ln_b[j],
                             sg_w_sp[j], sg_b_sp[j], ev_w_out[j])
        else:
            lambda_init = 0.8 - 0.6 * math.exp(-0.3 * i)
            x2 = _odd_layer(x2, bsz, seq, od_norm[j], od_w_in[j], da_lq1[j], da_lk1[j], da_lq2[j],
                            da_lk2[j], da_subln[j], od_w_out[j], lambda_init, final_norm,
                            final_norm=(i == depth - 1))
    return x2.reshape(bsz, seq, d)
```

```python
import functools
import math

import jax
import jax.numpy as jnp
from jax import lax
from jax.experimental import pallas as pl
from jax.experimental.pallas import tpu as pltpu

F32 = jnp.float32
BF16 = jnp.bfloat16

RMS_EPS = 1e-6
LN_EPS = 1e-5

SSM_GROUP = 16
SSM_STATE = 64
SSM_CHUNK = 16
S5_TILE_GROUPS = 16
SG_CHUNK = 128
SG_HEADS = 8
DA_HEADS = 8
DA_HEAD_DIM = 128
DA_V_DIM = 256
ROT_DIM = 32
ROPE_THETA = 500000.0
NEG_BIG = -1e30
ATTN_TILE = 512

VMEM_LIMIT_BYTES = 56 * 1024 * 1024


def _cparams(*sem):
    return pltpu.CompilerParams(dimension_semantics=sem, vmem_limit_bytes=VMEM_LIMIT_BYTES)


def _gelu_tanh(x):
    return 0.5 * x * (1.0 + jnp.tanh(math.sqrt(2.0 / math.pi) * (x + 0.044715 * (x * x * x))))


def _sigmoid(x):
    return 1.0 / (1.0 + jnp.exp(-x))


def _silu(x):
    return x * _sigmoid(x)


def _in_proj_body(x_ref, g_ref, w_ref, *rest, epilogues, n_aux, chunk_major_first):
    aux = rest[:n_aux]
    outs = rest[n_aux:n_aux + len(epilogues)]
    hn_ref = rest[n_aux + len(epilogues)]
    j = pl.program_id(1)

    @pl.when(j == 0)
    def _():
        x = x_ref[...]
        ms = jnp.mean(x * x, axis=-1, keepdims=True)
        hn_ref[...] = ((x * lax.rsqrt(ms + RMS_EPS)) * g_ref[...]).astype(BF16)

    acc = jnp.dot(hn_ref[...], w_ref[...], preferred_element_type=F32)
    for k, ep in enumerate(epilogues):
        @pl.when(j == k)
        def _(k=k, ep=ep):
            if k == 0 and chunk_major_first:
                tmp_ref = rest[-1]
                val = ep(acc, aux)
                n_blk, n_rows = tmp_ref.shape[0], tmp_ref.shape[1] // SSM_CHUNK
                for c in range(n_blk):
                    tmp_ref[c] = val[:, c * 128:(c + 1) * 128]
                for i in range(SSM_CHUNK):
                    rows = [tmp_ref[c, pl.ds(i, n_rows, stride=SSM_CHUNK), :] for c in range(n_blk)]
                    outs[0][i] = jnp.concatenate(rows, axis=1).astype(outs[0].dtype)
            else:
                outs[k][...] = ep(acc, aux).astype(outs[k].dtype)


def _in_proj(x2, norm_g, w_bf16, epilogues, out_dtypes, aux=(), aux_specs=(), *, tm, tn, name,
             chunk_major_first=False):
    n_tok, d = x2.shape
    n_seg = len(epilogues)
    assert w_bf16.shape == (d, n_seg * tn) and n_tok % tm == 0
    out_shape = [jax.ShapeDtypeStruct((n_tok, tn), dt) for dt in out_dtypes]
    out_specs = [pl.BlockSpec((tm, tn), lambda i, j: (i, 0)) for _ in range(n_seg)]
    scratch = [pltpu.VMEM((tm, d), BF16)]
    if chunk_major_first:
        t = SSM_CHUNK
        out_shape[0] = jax.ShapeDtypeStruct((t, n_tok // t, tn), out_dtypes[0])
        out_specs[0] = pl.BlockSpec((t, tm // t, tn), lambda i, j: (0, i, 0))
        scratch.append(pltpu.VMEM((tn // 128, tm, 128), F32))
    return pl.pallas_call(
        functools.partial(_in_proj_body, epilogues=epilogues, n_aux=len(aux),
                          chunk_major_first=chunk_major_first),
        grid=(n_tok // tm, n_seg),
        in_specs=[pl.BlockSpec((tm, d), lambda i, j: (i, 0)),
                  pl.BlockSpec((1, d), lambda i, j: (0, 0)),
                  pl.BlockSpec((d, tn), lambda i, j: (0, j))] + list(aux_specs),
        out_specs=out_specs,
        out_shape=out_shape,
        scratch_shapes=scratch,
        compiler_params=_cparams("parallel", "arbitrary"),
        name=name,
    )(x2, norm_g.reshape(1, d), w_bf16, *aux)


def _ep_identity(acc, aux):
    return acc


def _ep_silu(acc, aux):
    return _silu(acc)


def _ep_gelu(acc, aux):
    return _gelu_tanh(acc)


def _ep_gelu_layernorm(acc, aux):
    ln_g, ln_b = aux[0][...], aux[1][...]
    z = _gelu_tanh(acc)
    mu = jnp.mean(z, axis=-1, keepdims=True)
    zc = z - mu
    y = zc * lax.rsqrt(jnp.mean(zc * zc, axis=-1, keepdims=True) + LN_EPS)
    return y * ln_g + ln_b


def _rope(acc, tab):
    c, s_lo, s_hi = tab[:, 0:128], tab[:, 128:256], tab[:, 256:384]
    pieces = []
    for h in range(acc.shape[1] // DA_HEAD_DIM):
        blk = acc[:, h * DA_HEAD_DIM:(h + 1) * DA_HEAD_DIM]
        rot = (blk * c + pltpu.roll(blk, ROT_DIM // 2, 1) * s_lo
               + pltpu.roll(blk, DA_HEAD_DIM - ROT_DIM // 2, 1) * s_hi)
        pieces.append(rot)
    return jnp.concatenate(pieces, axis=1)


def _ep_rope_scaled(acc, aux):
    return _rope(acc, aux[0][...]) * (DA_HEAD_DIM ** -0.5 * math.log2(math.e))


def _ep_rope(acc, aux):
    return _rope(acc, aux[0][...])


def _rope_table(seq):
    pos = jnp.arange(seq, dtype=F32)
    inv_freq = ROPE_THETA ** (-jnp.arange(0, ROT_DIM, 2, dtype=F32) / ROT_DIM)
    ang = pos[:, None] * inv_freq[None, :]
    cos, sin = jnp.cos(ang), jnp.sin(ang)
    half = ROT_DIM // 2
    pad = DA_HEAD_DIM - ROT_DIM
    c = jnp.concatenate([cos, cos, jnp.ones((seq, pad), F32)], axis=1)
    s_lo = jnp.concatenate([jnp.zeros((seq, half), F32), sin, jnp.zeros((seq, pad), F32)], axis=1)
    s_hi = jnp.concatenate([-sin, jnp.zeros((seq, half + pad), F32)], axis=1)
    return jnp.concatenate([c, s_lo, s_hi], axis=1)


def _s5_prep_body(lr_ref, li_ref, ldt_ref, bre_ref, bim_ref, bbre_ref, bbim_ref, a_ref, pw_ref, *, n_pow):
    lr, li = lr_ref[...], li_ref[...]
    dt = jnp.exp(ldt_ref[...])
    mag = jnp.exp(lr * dt)
    a_re = mag * jnp.cos(li * dt)
    a_im = mag * jnp.sin(li * dt)
    den = lr * lr + li * li
    nr = a_re - 1.0
    f_re = (nr * lr + a_im * li) / den
    f_im = (a_im * lr - nr * li) / den
    b_re, b_im = bre_ref[...], bim_ref[...]
    bbre_ref[...] = f_re * b_re - f_im * b_im
    bbim_ref[...] = f_re * b_im + f_im * b_re
    a_ref[0:1, :] = a_re
    a_ref[1:2, :] = a_im
    mag_t = jnp.exp(lr * dt * SSM_CHUNK)
    pr = mag_t * jnp.cos(li * dt * SSM_CHUNK)
    pi_ = mag_t * jnp.sin(li * dt * SSM_CHUNK)
    for k in range(n_pow):
        pw_ref[k:k + 1, :] = pr
        pw_ref[n_pow + k:n_pow + k + 1, :] = pi_
        pr, pi_ = pr * pr - pi_ * pi_, 2.0 * pr * pi_


def _s5_prep(lam_re, lam_im, log_dt, b_re, b_im, n_pow):
    g, p = lam_re.shape
    h = b_re.shape[2]
    row = lambda a: a.reshape(1, g * p)
    chan = lambda b: b.transpose(2, 0, 1).reshape(h, g * p)
    ins = [row(lam_re), row(lam_im), row(jnp.broadcast_to(log_dt[:, None], (g, p))), chan(b_re), chan(b_im)]
    out_shape = [jax.ShapeDtypeStruct((h, g * p), F32), jax.ShapeDtypeStruct((h, g * p), F32),
                 jax.ShapeDtypeStruct((2, g * p), F32), jax.ShapeDtypeStruct((2 * n_pow, g * p), F32)]
    return pl.pallas_call(
        functools.partial(_s5_prep_body, n_pow=n_pow),
        out_shape=out_shape,
        compiler_params=pltpu.CompilerParams(vmem_limit_bytes=VMEM_LIMIT_BYTES),
        name="s5_prep",
    )(*ins)


def _block_diag(w4):
    eye = jnp.eye(w4.shape[1], dtype=w4.dtype)
    return w4[:, :, :, None, :] * eye[None, :, None, :, None]


def _s5_operators(bb_re, bb_im, a, pw, c_re, c_im, d_skip, n_pow):
    g, h, p = c_re.shape
    gt = S5_TILE_GROUPS
    nq = g // gt
    sw = gt * p

    def b_blk(bb):
        w4 = bb.reshape(h, nq, gt, p).transpose(1, 2, 0, 3)
        return _block_diag(w4).reshape(nq, gt * h, sw)

    def c_blk(c):
        w4 = c.reshape(nq, gt, h, p)
        return _block_diag(w4).transpose(0, 3, 4, 1, 2).reshape(nq, sw, gt * h)

    bblk = jnp.concatenate([b_blk(bb_re), b_blk(bb_im)], axis=2).astype(BF16)
    cblk = jnp.concatenate([c_blk(c_re), -c_blk(c_im)], axis=1).astype(BF16)
    a_rows = a.reshape(2, nq, sw).transpose(1, 0, 2)
    pw_rows = pw.reshape(2 * n_pow, nq, sw).transpose(1, 0, 2)
    d_rows = d_skip.reshape(nq, 1, gt * h)
    return bblk, cblk, a_rows, pw_rows, d_rows


def _s5_body(x_ref, bblk_ref, cblk_ref, a_ref, pw_ref, d_ref, y_ref, h_ref, *, n_pow):
    t = x_ref.shape[0]
    n_rows = x_ref.shape[1]
    half = h_ref.shape[1] // 2
    a_re, a_im = a_ref[0, 0:1, :], a_ref[0, 1:2, :]

    def advance(i):
        bu = jnp.dot(x_ref[i], bblk_ref[0], preferred_element_type=F32)
        h_re, h_im = h_ref[:, 0:half], h_ref[:, half:]
        n_re = a_re * h_re - a_im * h_im + bu[:, 0:half]
        n_im = a_re * h_im + a_im * h_re + bu[:, half:]
        h_ref[:, 0:half] = n_re
        h_ref[:, half:] = n_im
        return n_re, n_im

    h_ref[...] = jnp.zeros(h_ref.shape, F32)

    def local_step(i, c):
        advance(i)
        return c

    lax.fori_loop(0, t, local_step, 0)

    s_re, s_im = h_ref[:, 0:half], h_ref[:, half:]
    cidx = lax.broadcasted_iota(jnp.int32, (n_rows, half), 0)
    for k in range(n_pow):
        d = 1 << k
        p_re = jnp.where(cidx >= d, pltpu.roll(s_re, d, 0), 0.0)
        p_im = jnp.where(cidx >= d, pltpu.roll(s_im, d, 0), 0.0)
        w_re, w_im = pw_ref[0, k:k + 1, :], pw_ref[0, n_pow + k:n_pow + k + 1, :]
        s_re, s_im = s_re + w_re * p_re - w_im * p_im, s_im + w_re * p_im + w_im * p_re
    h_ref[:, 0:half] = jnp.where(cidx >= 1, pltpu.roll(s_re, 1, 0), 0.0)
    h_ref[:, half:] = jnp.where(cidx >= 1, pltpu.roll(s_im, 1, 0), 0.0)

    def emit_step(i, c):
        n_re, n_im = advance(i)
        hb = jnp.concatenate([n_re, n_im], axis=1).astype(BF16)
        y = jnp.dot(hb, cblk_ref[0], preferred_element_type=F32)
        y_ref[i] = y + d_ref[0] * x_ref[i].astype(F32)
        return c

    lax.fori_loop(0, t, emit_step, 0)


def _s5(xa_cm, bblk, cblk, a_rows, pw_rows, d_rows, bsz, n_pow):
    t, n_rows, w = xa_cm.shape
    nq, tile, s2 = bblk.shape
    rows_b = n_rows // bsz
    per_q = lambda a: pl.BlockSpec((1,) + a.shape[1:], lambda q, b: (q, 0, 0))
    act = pl.BlockSpec((t, rows_b, tile), lambda q, b: (0, b, q))
    return pl.pallas_call(
        functools.partial(_s5_body, n_pow=n_pow),
        grid=(nq, bsz),
        in_specs=[act, per_q(bblk), per_q(cblk), per_q(a_rows), per_q(pw_rows), per_q(d_rows)],
        out_specs=act,
        out_shape=jax.ShapeDtypeStruct((t, n_rows, w), F32),
        scratch_shapes=[pltpu.VMEM((rows_b, s2), F32)],
        compiler_params=_cparams("parallel", "parallel"),
        name="s5_scan",
    )(xa_cm, bblk, cblk, a_rows, pw_rows, d_rows)


def _even_out_body(y_ref, ga_ref, u_ref, v_ref, gb_ref, x_ref, wglu_ref, bglu_ref, wsp_ref, bsp_ref,
                   wout_ref, o_ref, mix_ref, ytok_ref, *, tm):
    n_blk = ytok_ref.shape[0]
    w = n_blk * 128
    for i in range(SSM_CHUNK):
        for c in range(n_blk):
            ytok_ref[c, pl.ds(i, tm // SSM_CHUNK, stride=SSM_CHUNK), :] = y_ref[i, :, c * 128:(c + 1) * 128]
    ya = _gelu_tanh(jnp.concatenate([ytok_ref[c] for c in range(n_blk)], axis=1))
    z = jnp.dot(ya.astype(BF16), wglu_ref[...], preferred_element_type=F32) + bglu_ref[...]
    ya = ya * _sigmoid(z)
    ya = ya * ga_ref[...].astype(F32)
    mix_ref[:, 0:w] = ya.astype(BF16)

    tri = (lax.broadcasted_iota(jnp.int32, (SG_CHUNK, SG_CHUNK), 1)
           <= lax.broadcasted_iota(jnp.int32, (SG_CHUNK, SG_CHUNK), 0))
    hd = w // SG_HEADS
    for g in range(SG_HEADS):
        w_c = jnp.where(tri, wsp_ref[g], 0.0).astype(BF16)
        bias = bsp_ref[:, g:g + 1]
        for n in range(tm // SG_CHUNK):
            rs = slice(n * SG_CHUNK, (n + 1) * SG_CHUNK)
            cs = slice(g * hd, (g + 1) * hd)
            s = jnp.dot(w_c, v_ref[rs, cs], preferred_element_type=F32) + bias
            yb = u_ref[rs, cs].astype(F32) * s * gb_ref[rs, cs].astype(F32)
            mix_ref[rs, w + g * hd:w + (g + 1) * hd] = yb.astype(BF16)

    o_ref[...] = x_ref[...] + jnp.dot(mix_ref[...], wout_ref[...], preferred_element_type=F32)


def _even_out(y, ga, u, v, gb, x2, w_glu, b_glu, w_sp, b_sp_t, w_out, *, tm):
    n_tok, d = x2.shape
    t, _, w = y.shape
    tok = lambda width: pl.BlockSpec((tm, width), lambda i: (i, 0))
    full = lambda a: pl.BlockSpec(a.shape, lambda i: (0,) * a.ndim)
    return pl.pallas_call(
        functools.partial(_even_out_body, tm=tm),
        grid=(n_tok // tm,),
        in_specs=[pl.BlockSpec((t, tm // t, w), lambda i: (0, i, 0)), tok(w), tok(w), tok(w), tok(w), tok(d),
                  full(w_glu), full(b_glu), full(w_sp), full(b_sp_t), full(w_out)],
        out_specs=tok(d),
        out_shape=jax.ShapeDtypeStruct((n_tok, d), F32),
        scratch_shapes=[pltpu.VMEM((tm, 2 * w), BF16), pltpu.VMEM((w // 128, tm, 128), F32)],
        compiler_params=_cparams("parallel"),
        name="even_out",
    )(y, ga, u, v, gb, x2, w_glu, b_glu, w_sp, b_sp_t, w_out)


def _attn_body(lq1_ref, lk1_ref, lq2_ref, lk2_ref, sg_ref, q_ref, k_ref, v_ref, o_ref,
               s_ref, mx_ref, mrep_ref, ls_ref, acc_ref, *, tq, lambda_init):
    qi = pl.program_id(2)
    hd = DA_HEAD_DIM
    n_lane = tq // 128
    lam = (jnp.exp(jnp.sum(lq1_ref[...] * lk1_ref[...])) - jnp.exp(jnp.sum(lq2_ref[...] * lk2_ref[...]))
           + lambda_init)
    q = q_ref[0]
    qs = (q[:, 0:hd], q[:, hd:2 * hd])
    mx_ref[...] = jnp.full(mx_ref.shape, NEG_BIG, F32)
    ls_ref[...] = jnp.zeros(ls_ref.shape, F32)
    acc_ref[...] = jnp.zeros(acc_ref.shape, F32)

    def lane_fold(x, op):
        out = x[:, 0:128]
        for c in range(1, n_lane):
            out = op(out, x[:, c * 128:(c + 1) * 128])
        return out

    def scores(j, masked):
        k0 = pl.multiple_of(j * tq, tq)
        kj = k_ref[0, pl.ds(k0, tq), :]
        for m in range(2):
            s = lax.dot_general(qs[m], kj[:, m * hd:(m + 1) * hd], (((1,), (1,)), ((), ())),
                                preferred_element_type=F32)
            if masked:
                keep = (lax.broadcasted_iota(jnp.int32, (tq, tq), 1)
                        <= lax.broadcasted_iota(jnp.int32, (tq, tq), 0))
                s = jnp.where(keep, s, NEG_BIG)
            s_ref[m, j] = s
            mx_ref[m] = jnp.maximum(mx_ref[m], lane_fold(s, jnp.maximum))

    def scores_step(j, c):
        scores(j, False)
        return c

    lax.fori_loop(0, qi, scores_step, 0)
    scores(qi, True)

    for m in range(2):
        mrep_ref[m] = jnp.broadcast_to(jnp.max(mx_ref[m], axis=-1, keepdims=True), (tq, 128))

    def accumulate(j, c):
        k0 = pl.multiple_of(j * tq, tq)
        vj = v_ref[0, pl.ds(k0, tq), :]
        for m in range(2):
            s = s_ref[m, j]
            mrep = mrep_ref[m]
            p = jnp.concatenate([jnp.exp2(s[:, c_ * 128:(c_ + 1) * 128] - mrep) for c_ in range(n_lane)],
                                axis=1)
            ls_ref[m] += lane_fold(p, jnp.add)
            acc_ref[m] += jnp.dot(p.astype(BF16), vj, preferred_element_type=F32)
        return c

    lax.fori_loop(0, qi + 1, accumulate, 0)

    o1 = acc_ref[0] / jnp.sum(ls_ref[0], axis=-1, keepdims=True)
    o2 = acc_ref[1] / jnp.sum(ls_ref[1], axis=-1, keepdims=True)
    o = o1 - lam * o2
    ms = jnp.mean(o * o, axis=-1, keepdims=True)
    o = (o * lax.rsqrt(ms + RMS_EPS)) * sg_ref[...]
    o_ref[0] = (o * (1.0 - lambda_init)).astype(o_ref.dtype)


def _diff_attn(q, k, v, lq1, lk1, lq2, lk2, subln_g, lambda_init, *, tq):
    b, seq, width = q.shape
    vec = lambda a: a.reshape(1, -1)
    small = lambda n: pl.BlockSpec((1, n), lambda bi, hi, qi: (0, 0))
    scratch = [pltpu.VMEM((2, seq // tq, tq, tq), F32),
               pltpu.VMEM((2, tq, 128), F32),
               pltpu.VMEM((2, tq, 128), F32),
               pltpu.VMEM((2, tq, 128), F32),
               pltpu.VMEM((2, tq, DA_V_DIM), F32)]
    return pl.pallas_call(
        functools.partial(_attn_body, tq=tq, lambda_init=lambda_init),
        grid=(b, DA_HEADS, seq // tq),
        in_specs=[small(DA_HEAD_DIM)] * 4 + [small(DA_V_DIM)] + [
            pl.BlockSpec((1, tq, DA_V_DIM), lambda bi, hi, qi: (bi, qi, hi)),
            pl.BlockSpec((1, seq, DA_V_DIM), lambda bi, hi, qi: (bi, 0, hi)),
            pl.BlockSpec((1, seq, DA_V_DIM), lambda bi, hi, qi: (bi, 0, hi))],
        out_specs=pl.BlockSpec((1, tq, DA_V_DIM), lambda bi, hi, qi: (bi, qi, hi)),
        out_shape=jax.ShapeDtypeStruct((b, seq, width), BF16),
        scratch_shapes=scratch,
        compiler_params=_cparams("parallel", "parallel", "arbitrary"),
        name="diff_attn",
    )(vec(lq1), vec(lk1), vec(lq2), vec(lk2), vec(subln_g), q, k, v)


def _odd_out_body(o_ref, g_ref, x_ref, w_ref, fn_ref, out_ref, *, final_norm):
    gated = (o_ref[...].astype(F32) * g_ref[...].astype(F32)).astype(BF16)
    y = x_ref[...] + jnp.dot(gated, w_ref[...], preferred_element_type=F32)
    if final_norm:
        ms = jnp.mean(y * y, axis=-1, keepdims=True)
        y = (y * lax.rsqrt(ms + RMS_EPS)) * fn_ref[...]
    out_ref[...] = y


def _odd_out(o, g, x2, w_out, final_g, *, tm, final_norm):
    n_tok, d = x2.shape
    tok = lambda dt: pl.BlockSpec((tm, d), lambda i: (i, 0))
    return pl.pallas_call(
        functools.partial(_odd_out_body, final_norm=final_norm),
        grid=(n_tok // tm,),
        in_specs=[tok(BF16), tok(BF16), tok(F32),
                  pl.BlockSpec(w_out.shape, lambda i: (0, 0)),
                  pl.BlockSpec((1, d), lambda i: (0, 0))],
        out_specs=tok(F32),
        out_shape=jax.ShapeDtypeStruct((n_tok, d), F32),
        compiler_params=_cparams("parallel"),
        name="odd_out",
    )(o, g, x2, w_out, final_g.reshape(1, d))


def _even_layer(x2, bsz, seq, norm_g, w_in, lam_re, lam_im, log_dt, b_re, b_im, c_re, c_im, d_skip,
                w_glu, b_glu, ln_g, ln_b, w_sp, b_sp, w_out):
    n_tok, d = x2.shape
    w = d // 2
    n_chunks = seq // SSM_CHUNK
    n_pow = max(1, (n_chunks - 1).bit_length())

    full = lambda a: pl.BlockSpec(a.shape, lambda i, j: (0, 0))
    ln_g2, ln_b2 = ln_g.reshape(1, w), ln_b.reshape(1, w)
    xa_cm, ga, u, v, gb = _in_proj(
        x2, norm_g, w_in.astype(BF16),
        [_ep_identity, _ep_silu, _ep_gelu, _ep_gelu_layernorm, _ep_silu],
        [BF16] * 5, aux=(ln_g2, ln_b2), aux_specs=(full(ln_g2), full(ln_b2)),
        tm=512, tn=w, name="even_in_proj", chunk_major_first=True)

    bb_re, bb_im, a, pw = _s5_prep(lam_re, lam_im, log_dt, b_re, b_im, n_pow)
    ops = _s5_operators(bb_re, bb_im, a, pw, c_re, c_im, d_skip, n_pow)
    y = _s5(xa_cm, *ops, bsz, n_pow)

    return _even_out(y, ga, u, v, gb, x2, w_glu.astype(BF16), b_glu.reshape(1, w),
                     w_sp, b_sp.T, w_out.astype(BF16), tm=512)


def _odd_layer(x2, bsz, seq, norm_g, w_in, lq1, lk1, lq2, lk2, subln_g, w_out, lambda_init,
               final_g, final_norm):
    n_tok, d = x2.shape
    tm = 512
    tab = _rope_table(seq)
    n_pos_blocks = seq // tm
    tab_spec = pl.BlockSpec((tm, tab.shape[1]), lambda i, j: (i % n_pos_blocks, 0))
    q, k, v, g = _in_proj(
        x2, norm_g, w_in.astype(BF16),
        [_ep_rope_scaled, _ep_rope, _ep_identity, _ep_silu],
        [BF16] * 4, aux=(tab,), aux_specs=(tab_spec,), tm=tm, tn=d, name="odd_in_proj")
    shp = (bsz, seq, d)
    o = _diff_attn(q.reshape(shp), k.reshape(shp), v.reshape(shp), lq1, lk1, lq2, lk2, subln_g,
                   lambda_init, tq=ATTN_TILE)
    return _odd_out(o.reshape(n_tok, d), g, x2, w_out.astype(BF16), final_g, tm=512,
                    final_norm=final_norm)


def kernel(x, ev_norm, ev_w_in, ssm_lam_re, ssm_lam_im, ssm_log_dt, ssm_b_re, ssm_b_im, ssm_c_re, ssm_c_im, ssm_d, ssm_w_glu, ssm_b_glu, sg_ln_g, sg_ln_b, sg_w_sp, sg_b_sp, ev_w_out, od_norm, od_w_in, da_lq1, da_lk1, da_lq2, da_lk2, da_subln, od_w_out, final_norm):
    bsz, seq, d = x.shape
    depth = ev_norm.shape[0] + od_norm.shape[0]
    assert depth % 2 == 0, "the final rmsnorm is fused into the last (odd) layer"
    x2 = x.reshape(bsz * seq, d)
    for i in range(depth):
        j = i // 2
        if i % 2 == 0:
            x2 = _even_layer(x2, bsz, seq, ev_norm[j], ev_w_in[j], ssm_lam_re[j], ssm_lam_im[j],
                             ssm_log_dt[j], ssm_b_re[j], ssm_b_im[j], ssm_c_re[j], ssm_c_im[j],
                             ssm_d[j], ssm_w_glu[j], ssm_b_glu[j], sg_ln_g[j], sg_ln_b[j],
                             sg_w_sp[j], sg_b_sp[j], ev_w_out[j])
        else:
            lambda_init = 0.8 - 0.6 * math.exp(-0.3 * i)
            x2 = _odd_layer(x2, bsz, seq, od_norm[j], od_w_in[j], da_lq1[j], da_lk1[j], da_lq2[j],
                            da_lk2[j], da_subln[j], od_w_out[j], lambda_init, final_norm,
                            final_norm=(i == depth - 1))
    return x2.reshape(bsz, seq, d)
```

```python
import functools
import math

import jax
import jax.numpy as jnp
from jax import lax
from jax.experimental import pallas as pl
from jax.experimental.pallas import tpu as pltpu

F32 = jnp.float32
BF16 = jnp.bfloat16

RMS_EPS = 1e-6
LN_EPS = 1e-5

SSM_GROUP = 16
SSM_STATE = 64
SSM_CHUNK = 16
S5_TILE_GROUPS = 16
SG_CHUNK = 128
SG_HEADS = 8
DA_HEADS = 8
DA_HEAD_DIM = 128
DA_V_DIM = 256
ROT_DIM = 32
ROPE_THETA = 500000.0
NEG_BIG = -1e30
ATTN_TILE = 512
IN_PROJ_COL_CHUNK = 512

VMEM_LIMIT_BYTES = 56 * 1024 * 1024


def _cparams(*sem):
    return pltpu.CompilerParams(dimension_semantics=sem, vmem_limit_bytes=VMEM_LIMIT_BYTES)


def _gelu_tanh(x):
    return 0.5 * x * (1.0 + jnp.tanh(math.sqrt(2.0 / math.pi) * (x + 0.044715 * (x * x * x))))


def _sigmoid(x):
    return 1.0 / (1.0 + jnp.exp(-x))


def _silu(x):
    return x * _sigmoid(x)


def _in_proj_body(x_ref, g_ref, w_ref, *rest, epilogues, n_aux, chunk_major_first):
    aux = rest[:n_aux]
    outs = rest[n_aux:n_aux + len(epilogues)]
    hn_ref = rest[n_aux + len(epilogues)]
    j = pl.program_id(1)

    @pl.when(j == 0)
    def _():
        x = x_ref[...]
        ms = jnp.mean(x * x, axis=-1, keepdims=True)
        hn_ref[...] = ((x * lax.rsqrt(ms + RMS_EPS)) * g_ref[...]).astype(BF16)

    tn = w_ref.shape[1]
    for k, (ep, col_chunk) in enumerate(epilogues):
        @pl.when(j == k)
        def _(k=k, ep=ep, cw=col_chunk or tn):
            chunk_major = k == 0 and chunk_major_first
            for c0 in range(0, tn, cw):
                acc = jnp.dot(hn_ref[...], w_ref[:, c0:c0 + cw], preferred_element_type=F32)
                val = ep(acc, aux)
                if chunk_major:
                    tmp_ref = rest[-1]
                    for c in range(cw // 128):
                        tmp_ref[c0 // 128 + c] = val[:, c * 128:(c + 1) * 128]
                else:
                    outs[k][:, c0:c0 + cw] = val.astype(outs[k].dtype)
            if chunk_major:
                n_blk, n_rows = tmp_ref.shape[0], tmp_ref.shape[1] // SSM_CHUNK
                for i in range(SSM_CHUNK):
                    rows = [tmp_ref[c, pl.ds(i, n_rows, stride=SSM_CHUNK), :] for c in range(n_blk)]
                    outs[0][i] = jnp.concatenate(rows, axis=1).astype(outs[0].dtype)


def _in_proj(x2, norm_g, w_bf16, epilogues, out_dtypes, aux=(), aux_specs=(), *, tm, tn, name,
             chunk_major_first=False):
    n_tok, d = x2.shape
    n_seg = len(epilogues)
    assert w_bf16.shape == (d, n_seg * tn) and n_tok % tm == 0
    out_shape = [jax.ShapeDtypeStruct((n_tok, tn), dt) for dt in out_dtypes]
    out_specs = [pl.BlockSpec((tm, tn), lambda i, j: (i, 0)) for _ in range(n_seg)]
    scratch = [pltpu.VMEM((tm, d), BF16)]
    if chunk_major_first:
        t = SSM_CHUNK
        out_shape[0] = jax.ShapeDtypeStruct((t, n_tok // t, tn), out_dtypes[0])
        out_specs[0] = pl.BlockSpec((t, tm // t, tn), lambda i, j: (0, i, 0))
        scratch.append(pltpu.VMEM((tn // 128, tm, 128), F32))
    return pl.pallas_call(
        functools.partial(_in_proj_body, epilogues=epilogues, n_aux=len(aux),
                          chunk_major_first=chunk_major_first),
        grid=(n_tok // tm, n_seg),
        in_specs=[pl.BlockSpec((tm, d), lambda i, j: (i, 0)),
                  pl.BlockSpec((1, d), lambda i, j: (0, 0)),
                  pl.BlockSpec((d, tn), lambda i, j: (0, j))] + list(aux_specs),
        out_specs=out_specs,
        out_shape=out_shape,
        scratch_shapes=scratch,
        compiler_params=_cparams("parallel", "arbitrary"),
        name=name,
    )(x2, norm_g.reshape(1, d), w_bf16, *aux)


def _ep_identity(acc, aux):
    return acc


def _ep_silu(acc, aux):
    return _silu(acc)


def _ep_gelu(acc, aux):
    return _gelu_tanh(acc)


def _ep_gelu_layernorm(acc, aux):
    ln_g, ln_b = aux[0][...], aux[1][...]
    z = _gelu_tanh(acc)
    mu = jnp.mean(z, axis=-1, keepdims=True)
    zc = z - mu
    y = zc * lax.rsqrt(jnp.mean(zc * zc, axis=-1, keepdims=True) + LN_EPS)
    return y * ln_g + ln_b


def _rope(acc, tab):
    c, s_lo, s_hi = tab[:, 0:128], tab[:, 128:256], tab[:, 256:384]
    pieces = []
    for h in range(acc.shape[1] // DA_HEAD_DIM):
        blk = acc[:, h * DA_HEAD_DIM:(h + 1) * DA_HEAD_DIM]
        rot = (blk * c + pltpu.roll(blk, ROT_DIM // 2, 1) * s_lo
               + pltpu.roll(blk, DA_HEAD_DIM - ROT_DIM // 2, 1) * s_hi)
        pieces.append(rot)
    return jnp.concatenate(pieces, axis=1)


def _ep_rope_scaled(acc, aux):
    return _rope(acc, aux[0][...]) * (DA_HEAD_DIM ** -0.5 * math.log2(math.e))


def _ep_rope(acc, aux):
    return _rope(acc, aux[0][...])


def _rope_table(seq):
    pos = jnp.arange(seq, dtype=F32)
    inv_freq = ROPE_THETA ** (-jnp.arange(0, ROT_DIM, 2, dtype=F32) / ROT_DIM)
    ang = pos[:, None] * inv_freq[None, :]
    cos, sin = jnp.cos(ang), jnp.sin(ang)
    half = ROT_DIM // 2
    pad = DA_HEAD_DIM - ROT_DIM
    c = jnp.concatenate([cos, cos, jnp.ones((seq, pad), F32)], axis=1)
    s_lo = jnp.concatenate([jnp.zeros((seq, half), F32), sin, jnp.zeros((seq, pad), F32)], axis=1)
    s_hi = jnp.concatenate([-sin, jnp.zeros((seq, half + pad), F32)], axis=1)
    return jnp.concatenate([c, s_lo, s_hi], axis=1)


def _s5_prep_body(lr_ref, li_ref, ldt_ref, bre_ref, bim_ref, bbre_ref, bbim_ref, a_ref, pw_ref, *, n_pow):
    lr, li = lr_ref[...], li_ref[...]
    dt = jnp.exp(ldt_ref[...])
    mag = jnp.exp(lr * dt)
    a_re = mag * jnp.cos(li * dt)
    a_im = mag * jnp.sin(li * dt)
    den = lr * lr + li * li
    nr = a_re - 1.0
    f_re = (nr * lr + a_im * li) / den
    f_im = (a_im * lr - nr * li) / den
    b_re, b_im = bre_ref[...], bim_ref[...]
    bbre_ref[...] = f_re * b_re - f_im * b_im
    bbim_ref[...] = f_re * b_im + f_im * b_re
    a_ref[0:1, :] = a_re
    a_ref[1:2, :] = a_im
    mag_t = jnp.exp(lr * dt * SSM_CHUNK)
    pr = mag_t * jnp.cos(li * dt * SSM_CHUNK)
    pi_ = mag_t * jnp.sin(li * dt * SSM_CHUNK)
    for k in range(n_pow):
        pw_ref[k:k + 1, :] = pr
        pw_ref[n_pow + k:n_pow + k + 1, :] = pi_
        pr, pi_ = pr * pr - pi_ * pi_, 2.0 * pr * pi_


def _s5_prep(lam_re, lam_im, log_dt, b_re, b_im, n_pow):
    g, p = lam_re.shape
    h = b_re.shape[2]
    row = lambda a: a.reshape(1, g * p)
    chan = lambda b: b.transpose(2, 0, 1).reshape(h, g * p)
    ins = [row(lam_re), row(lam_im), row(jnp.broadcast_to(log_dt[:, None], (g, p))), chan(b_re), chan(b_im)]
    out_shape = [jax.ShapeDtypeStruct((h, g * p), F32), jax.ShapeDtypeStruct((h, g * p), F32),
                 jax.ShapeDtypeStruct((2, g * p), F32), jax.ShapeDtypeStruct((2 * n_pow, g * p), F32)]
    return pl.pallas_call(
        functools.partial(_s5_prep_body, n_pow=n_pow),
        out_shape=out_shape,
        compiler_params=pltpu.CompilerParams(vmem_limit_bytes=VMEM_LIMIT_BYTES),
        name="s5_prep",
    )(*ins)


def _block_diag(w4):
    eye = jnp.eye(w4.shape[1], dtype=w4.dtype)
    return w4[:, :, :, None, :] * eye[None, :, None, :, None]


def _s5_operators(bb_re, bb_im, a, pw, c_re, c_im, d_skip, n_pow):
    g, h, p = c_re.shape
    gt = S5_TILE_GROUPS
    nq = g // gt
    sw = gt * p

    def b_blk(bb):
        w4 = bb.reshape(h, nq, gt, p).transpose(1, 2, 0, 3)
        return _block_diag(w4).reshape(nq, gt * h, sw)

    def c_blk(c):
        w4 = c.reshape(nq, gt, h, p)
        return _block_diag(w4).transpose(0, 3, 4, 1, 2).reshape(nq, sw, gt * h)

    bblk = jnp.concatenate([b_blk(bb_re), b_blk(bb_im)], axis=2).astype(BF16)
    cblk = jnp.concatenate([c_blk(c_re), -c_blk(c_im)], axis=1).astype(BF16)
    a_rows = a.reshape(2, nq, sw).transpose(1, 0, 2)
    pw_rows = pw.reshape(2 * n_pow, nq, sw).transpose(1, 0, 2)
    d_rows = d_skip.reshape(nq, 1, gt * h)
    return bblk, cblk, a_rows, pw_rows, d_rows


def _s5_body(x_ref, bblk_ref, cblk_ref, a_ref, pw_ref, d_ref, y_ref, h_ref, *, n_pow):
    t = x_ref.shape[0]
    n_rows = x_ref.shape[1]
    half = h_ref.shape[1] // 2
    a_re, a_im = a_ref[0, 0:1, :], a_ref[0, 1:2, :]

    def advance(i):
        bu = jnp.dot(x_ref[i], bblk_ref[0], preferred_element_type=F32)
        h_re, h_im = h_ref[:, 0:half], h_ref[:, half:]
        n_re = a_re * h_re - a_im * h_im + bu[:, 0:half]
        n_im = a_re * h_im + a_im * h_re + bu[:, half:]
        h_ref[:, 0:half] = n_re
        h_ref[:, half:] = n_im
        return n_re, n_im

    h_ref[...] = jnp.zeros(h_ref.shape, F32)

    def local_step(i, c):
        advance(i)
        return c

    lax.fori_loop(0, t, local_step, 0)

    s_re, s_im = h_ref[:, 0:half], h_ref[:, half:]
    cidx = lax.broadcasted_iota(jnp.int32, (n_rows, half), 0)
    for k in range(n_pow):
        d = 1 << k
        p_re = jnp.where(cidx >= d, pltpu.roll(s_re, d, 0), 0.0)
        p_im = jnp.where(cidx >= d, pltpu.roll(s_im, d, 0), 0.0)
        w_re, w_im = pw_ref[0, k:k + 1, :], pw_ref[0, n_pow + k:n_pow + k + 1, :]
        s_re, s_im = s_re + w_re * p_re - w_im * p_im, s_im + w_re * p_im + w_im * p_re
    h_ref[:, 0:half] = jnp.where(cidx >= 1, pltpu.roll(s_re, 1, 0), 0.0)
    h_ref[:, half:] = jnp.where(cidx >= 1, pltpu.roll(s_im, 1, 0), 0.0)

    def emit_step(i, c):
        n_re, n_im = advance(i)
        hb = jnp.concatenate([n_re, n_im], axis=1).astype(BF16)
        y = jnp.dot(hb, cblk_ref[0], preferred_element_type=F32)
        y_ref[i] = y + d_ref[0] * x_ref[i].astype(F32)
        return c

    lax.fori_loop(0, t, emit_step, 0)


def _s5(xa_cm, bblk, cblk, a_rows, pw_rows, d_rows, bsz, n_pow):
    t, n_rows, w = xa_cm.shape
    nq, tile, s2 = bblk.shape
    rows_b = n_rows // bsz
    per_q = lambda a: pl.BlockSpec((1,) + a.shape[1:], lambda q, b: (q, 0, 0))
    act = pl.BlockSpec((t, rows_b, tile), lambda q, b: (0, b, q))
    return pl.pallas_call(
        functools.partial(_s5_body, n_pow=n_pow),
        grid=(nq, bsz),
        in_specs=[act, per_q(bblk), per_q(cblk), per_q(a_rows), per_q(pw_rows), per_q(d_rows)],
        out_specs=act,
        out_shape=jax.ShapeDtypeStruct((t, n_rows, w), F32),
        scratch_shapes=[pltpu.VMEM((rows_b, s2), F32)],
        compiler_params=_cparams("parallel", "parallel"),
        name="s5_scan",
    )(xa_cm, bblk, cblk, a_rows, pw_rows, d_rows)


def _even_out_body(y_ref, ga_ref, u_ref, v_ref, gb_ref, x_ref, wglu_ref, bglu_ref, wsp_ref, bsp_ref,
                   wout_ref, o_ref, mix_ref, ytok_ref, *, tm):
    n_blk = ytok_ref.shape[0]
    w = n_blk * 128
    for i in range(SSM_CHUNK):
        for c in range(n_blk):
            ytok_ref[c, pl.ds(i, tm // SSM_CHUNK, stride=SSM_CHUNK), :] = y_ref[i, :, c * 128:(c + 1) * 128]
    ya = _gelu_tanh(jnp.concatenate([ytok_ref[c] for c in range(n_blk)], axis=1))
    z = jnp.dot(ya.astype(BF16), wglu_ref[...], preferred_element_type=F32) + bglu_ref[...]
    ya = ya * _sigmoid(z)
    ya = ya * ga_ref[...].astype(F32)
    mix_ref[:, 0:w] = ya.astype(BF16)

    tri = (lax.broadcasted_iota(jnp.int32, (SG_CHUNK, SG_CHUNK), 1)
           <= lax.broadcasted_iota(jnp.int32, (SG_CHUNK, SG_CHUNK), 0))
    hd = w // SG_HEADS
    for g in range(SG_HEADS):
        w_c = jnp.where(tri, wsp_ref[g], 0.0).astype(BF16)
        bias = bsp_ref[:, g:g + 1]
        for n in range(tm // SG_CHUNK):
            rs = slice(n * SG_CHUNK, (n + 1) * SG_CHUNK)
            cs = slice(g * hd, (g + 1) * hd)
            s = jnp.dot(w_c, v_ref[rs, cs], preferred_element_type=F32) + bias
            yb = u_ref[rs, cs].astype(F32) * s * gb_ref[rs, cs].astype(F32)
            mix_ref[rs, w + g * hd:w + (g + 1) * hd] = yb.astype(BF16)

    o_ref[...] = x_ref[...] + jnp.dot(mix_ref[...], wout_ref[...], preferred_element_type=F32)


def _even_out(y, ga, u, v, gb, x2, w_glu, b_glu, w_sp, b_sp_t, w_out, *, tm):
    n_tok, d = x2.shape
    t, _, w = y.shape
    tok = lambda width: pl.BlockSpec((tm, width), lambda i: (i, 0))
    full = lambda a: pl.BlockSpec(a.shape, lambda i: (0,) * a.ndim)
    return pl.pallas_call(
        functools.partial(_even_out_body, tm=tm),
        grid=(n_tok // tm,),
        in_specs=[pl.BlockSpec((t, tm // t, w), lambda i: (0, i, 0)), tok(w), tok(w), tok(w), tok(w), tok(d),
                  full(w_glu), full(b_glu), full(w_sp), full(b_sp_t), full(w_out)],
        out_specs=tok(d),
        out_shape=jax.ShapeDtypeStruct((n_tok, d), F32),
        scratch_shapes=[pltpu.VMEM((tm, 2 * w), BF16), pltpu.VMEM((w // 128, tm, 128), F32)],
        compiler_params=_cparams("parallel"),
        name="even_out",
    )(y, ga, u, v, gb, x2, w_glu, b_glu, w_sp, b_sp_t, w_out)


def _attn_body(lq1_ref, lk1_ref, lq2_ref, lk2_ref, sg_ref, q_ref, k_ref, v_ref, o_ref,
               s_ref, mx_ref, mrep_ref, ls_ref, acc_ref, *, tq, lambda_init):
    qi = pl.program_id(2)
    hd = DA_HEAD_DIM
    n_lane = tq // 128
    lam = (jnp.exp(jnp.sum(lq1_ref[...] * lk1_ref[...])) - jnp.exp(jnp.sum(lq2_ref[...] * lk2_ref[...]))
           + lambda_init)
    q = q_ref[0]
    qs = (q[:, 0:hd], q[:, hd:2 * hd])
    mx_ref[...] = jnp.full(mx_ref.shape, NEG_BIG, F32)
    ls_ref[...] = jnp.zeros(ls_ref.shape, F32)
    acc_ref[...] = jnp.zeros(acc_ref.shape, F32)

    def lane_fold(x, op):
        out = x[:, 0:128]
        for c in range(1, n_lane):
            out = op(out, x[:, c * 128:(c + 1) * 128])
        return out

    def scores(j, masked):
        k0 = pl.multiple_of(j * tq, tq)
        kj = k_ref[0, pl.ds(k0, tq), :]
        for m in range(2):
            s = lax.dot_general(qs[m], kj[:, m * hd:(m + 1) * hd], (((1,), (1,)), ((), ())),
                                preferred_element_type=F32)
            if masked:
                keep = (lax.broadcasted_iota(jnp.int32, (tq, tq), 1)
                        <= lax.broadcasted_iota(jnp.int32, (tq, tq), 0))
                s = jnp.where(keep, s, NEG_BIG)
            s_ref[m, j] = s
            mx_ref[m] = jnp.maximum(mx_ref[m], lane_fold(s, jnp.maximum))

    def scores_pair(jp, c):
        scores(2 * jp, False)
        scores(2 * jp + 1, False)
        return c

    lax.fori_loop(0, qi // 2, scores_pair, 0)

    @pl.when(qi % 2 == 1)
    def _():
        scores(qi - 1, False)

    scores(qi, True)

    for m in range(2):
        mrep_ref[m] = jnp.broadcast_to(jnp.max(mx_ref[m], axis=-1, keepdims=True), (tq, 128))

    def accumulate(j):
        k0 = pl.multiple_of(j * tq, tq)
        vj = v_ref[0, pl.ds(k0, tq), :]
        for m in range(2):
            s = s_ref[m, j]
            mrep = mrep_ref[m]
            p = jnp.concatenate([jnp.exp2(s[:, c_ * 128:(c_ + 1) * 128] - mrep) for c_ in range(n_lane)],
                                axis=1)
            ls_ref[m] += lane_fold(p, jnp.add)
            acc_ref[m] += jnp.dot(p.astype(BF16), vj, preferred_element_type=F32)

    def accumulate_pair(jp, c):
        accumulate(2 * jp)
        accumulate(2 * jp + 1)
        return c

    lax.fori_loop(0, (qi + 1) // 2, accumulate_pair, 0)

    @pl.when(qi % 2 == 0)
    def _():
        accumulate(qi)

    o1 = acc_ref[0] / jnp.sum(ls_ref[0], axis=-1, keepdims=True)
    o2 = acc_ref[1] / jnp.sum(ls_ref[1], axis=-1, keepdims=True)
    o = o1 - lam * o2
    ms = jnp.mean(o * o, axis=-1, keepdims=True)
    o = (o * lax.rsqrt(ms + RMS_EPS)) * sg_ref[...]
    o_ref[0] = (o * (1.0 - lambda_init)).astype(o_ref.dtype)


def _diff_attn(q, k, v, lq1, lk1, lq2, lk2, subln_g, lambda_init, *, tq):
    b, seq, width = q.shape
    vec = lambda a: a.reshape(1, -1)
    small = lambda n: pl.BlockSpec((1, n), lambda bi, hi, qi: (0, 0))
    scratch = [pltpu.VMEM((2, seq // tq, tq, tq), F32),
               pltpu.VMEM((2, tq, 128), F32),
               pltpu.VMEM((2, tq, 128), F32),
               pltpu.VMEM((2, tq, 128), F32),
               pltpu.VMEM((2, tq, DA_V_DIM), F32)]
    return pl.pallas_call(
        functools.partial(_attn_body, tq=tq, lambda_init=lambda_init),
        grid=(b, DA_HEADS, seq // tq),
        in_specs=[small(DA_HEAD_DIM)] * 4 + [small(DA_V_DIM)] + [
            pl.BlockSpec((1, tq, DA_V_DIM), lambda bi, hi, qi: (bi, qi, hi)),
            pl.BlockSpec((1, seq, DA_V_DIM), lambda bi, hi, qi: (bi, 0, hi)),
            pl.BlockSpec((1, seq, DA_V_DIM), lambda bi, hi, qi: (bi, 0, hi))],
        out_specs=pl.BlockSpec((1, tq, DA_V_DIM), lambda bi, hi, qi: (bi, qi, hi)),
        out_shape=jax.ShapeDtypeStruct((b, seq, width), BF16),
        scratch_shapes=scratch,
        compiler_params=_cparams("parallel", "parallel", "arbitrary"),
        name="diff_attn",
    )(vec(lq1), vec(lk1), vec(lq2), vec(lk2), vec(subln_g), q, k, v)


def _odd_out_body(o_ref, g_ref, x_ref, w_ref, fn_ref, out_ref, *, final_norm):
    gated = (o_ref[...].astype(F32) * g_ref[...].astype(F32)).astype(BF16)
    y = x_ref[...] + jnp.dot(gated, w_ref[...], preferred_element_type=F32)
    if final_norm:
        ms = jnp.mean(y * y, axis=-1, keepdims=True)
        y = (y * lax.rsqrt(ms + RMS_EPS)) * fn_ref[...]
    out_ref[...] = y


def _odd_out(o, g, x2, w_out, final_g, *, tm, final_norm):
    n_tok, d = x2.shape
    tok = lambda dt: pl.BlockSpec((tm, d), lambda i: (i, 0))
    return pl.pallas_call(
        functools.partial(_odd_out_body, final_norm=final_norm),
        grid=(n_tok // tm,),
        in_specs=[tok(BF16), tok(BF16), tok(F32),
                  pl.BlockSpec(w_out.shape, lambda i: (0, 0)),
                  pl.BlockSpec((1, d), lambda i: (0, 0))],
        out_specs=tok(F32),
        out_shape=jax.ShapeDtypeStruct((n_tok, d), F32),
        compiler_params=_cparams("parallel"),
        name="odd_out",
    )(o, g, x2, w_out, final_g.reshape(1, d))


def _even_layer(x2, bsz, seq, norm_g, w_in, lam_re, lam_im, log_dt, b_re, b_im, c_re, c_im, d_skip,
                w_glu, b_glu, ln_g, ln_b, w_sp, b_sp, w_out):
    n_tok, d = x2.shape
    w = d // 2
    n_chunks = seq // SSM_CHUNK
    n_pow = max(1, (n_chunks - 1).bit_length())

    full = lambda a: pl.BlockSpec(a.shape, lambda i, j: (0, 0))
    ln_g2, ln_b2 = ln_g.reshape(1, w), ln_b.reshape(1, w)
    xa_cm, ga, u, v, gb = _in_proj(
        x2, norm_g, w_in.astype(BF16),
        [(_ep_identity, IN_PROJ_COL_CHUNK), (_ep_silu, IN_PROJ_COL_CHUNK), (_ep_gelu, IN_PROJ_COL_CHUNK),
         (_ep_gelu_layernorm, None), (_ep_silu, IN_PROJ_COL_CHUNK)],
        [BF16] * 5, aux=(ln_g2, ln_b2), aux_specs=(full(ln_g2), full(ln_b2)),
        tm=512, tn=w, name="even_in_proj", chunk_major_first=True)

    bb_re, bb_im, a, pw = _s5_prep(lam_re, lam_im, log_dt, b_re, b_im, n_pow)
    ops = _s5_operators(bb_re, bb_im, a, pw, c_re, c_im, d_skip, n_pow)
    y = _s5(xa_cm, *ops, bsz, n_pow)

    return _even_out(y, ga, u, v, gb, x2, w_glu.astype(BF16), b_glu.reshape(1, w),
                     w_sp, b_sp.T, w_out.astype(BF16), tm=512)


def _odd_layer(x2, bsz, seq, norm_g, w_in, lq1, lk1, lq2, lk2, subln_g, w_out, lambda_init,
               final_g, final_norm):
    n_tok, d = x2.shape
    tm = 512
    tab = _rope_table(seq)
    n_pos_blocks = seq // tm
    tab_spec = pl.BlockSpec((tm, tab.shape[1]), lambda i, j: (i % n_pos_blocks, 0))
    q, k, v, g = _in_proj(
        x2, norm_g, w_in.astype(BF16),
        [(_ep_rope_scaled, IN_PROJ_COL_CHUNK), (_ep_rope, IN_PROJ_COL_CHUNK),
         (_ep_identity, IN_PROJ_COL_CHUNK), (_ep_silu, IN_PROJ_COL_CHUNK)],
        [BF16] * 4, aux=(tab,), aux_specs=(tab_spec,), tm=tm, tn=d, name="odd_in_proj")
    shp = (bsz, seq, d)
    o = _diff_attn(q.reshape(shp), k.reshape(shp), v.reshape(shp), lq1, lk1, lq2, lk2, subln_g,
                   lambda_init, tq=ATTN_TILE)
    return _odd_out(o.reshape(n_tok, d), g, x2, w_out.astype(BF16), final_g, tm=512,
                    final_norm=final_norm)


def kernel(x, ev_norm, ev_w_in, ssm_lam_re, ssm_lam_im, ssm_log_dt, ssm_b_re, ssm_b_im, ssm_c_re, ssm_c_im, ssm_d, ssm_w_glu, ssm_b_glu, sg_ln_g, sg_ln_b, sg_w_sp, sg_b_sp, ev_w_out, od_norm, od_w_in, da_lq1, da_lk1, da_lq2, da_lk2, da_subln, od_w_out, final_norm):
    bsz, seq, d = x.shape
    depth = ev_norm.shape[0] + od_norm.shape[0]
    assert depth % 2 == 0, "the final rmsnorm is fused into the last (odd) layer"
    x2 = x.reshape(bsz * seq, d)
    for i in range(depth):
        j = i // 2
        if i % 2 == 0:
            x2 = _even_layer(x2, bsz, seq, ev_norm[j], ev_w_in[j], ssm_lam_re[j], ssm_lam_im[j],
                             ssm_log_dt[j], ssm_b_re[j], ssm_b_im[j], ssm_c_re[j], ssm_c_im[j],
                             ssm_d[j], ssm_w_glu[j], ssm_b_glu[j], sg_ln_g[j], sg_ln_b[j],
                             sg_w_sp[j], sg_b_sp[j], ev_w_out[j])
        else:
            lambda_init = 0.8 - 0.6 * math.exp(-0.3 * i)
            x2 = _odd_layer(x2, bsz, seq, od_norm[j], od_w_in[j], da_lq1[j], da_lk1[j], da_lq2[j],
                            da_lk2[j], da_subln[j], od_w_out[j], lambda_init, final_norm,
                            final_norm=(i == depth - 1))
    return x2.reshape(bsz, seq, d)
```

```python
import functools
import math

import jax
import jax.numpy as jnp
from jax import lax
from jax.experimental import pallas as pl
from jax.experimental.pallas import tpu as pltpu

F32 = jnp.float32
BF16 = jnp.bfloat16

RMS_EPS = 1e-6
LN_EPS = 1e-5

SSM_GROUP = 16
SSM_STATE = 64
SSM_CHUNK = 16
S5_TILE_GROUPS = 16
SG_CHUNK = 128
SG_HEADS = 8
DA_HEADS = 8
DA_HEAD_DIM = 128
DA_V_DIM = 256
ROT_DIM = 32
ROPE_THETA = 500000.0
NEG_BIG = -1e30
ATTN_TILE = 512
IN_PROJ_COL_CHUNK = 512

VMEM_LIMIT_BYTES = 56 * 1024 * 1024


def _cparams(*sem):
    return pltpu.CompilerParams(dimension_semantics=sem, vmem_limit_bytes=VMEM_LIMIT_BYTES)


def _gelu_tanh(x):
    return 0.5 * x * (1.0 + jnp.tanh(math.sqrt(2.0 / math.pi) * (x + 0.044715 * (x * x * x))))


def _sigmoid(x):
    return 1.0 / (1.0 + jnp.exp(-x))


def _silu(x):
    return x * _sigmoid(x)


def _in_proj_body(x_ref, g_ref, w_ref, *rest, epilogues, n_aux, chunk_major_first, tn, resident):
    aux = rest[:n_aux]
    outs = rest[n_aux:n_aux + len(epilogues)]
    hn_ref = rest[n_aux + len(epilogues)]
    j = pl.program_id(1)

    def normalise():
        x = x_ref[...]
        ms = jnp.mean(x * x, axis=-1, keepdims=True)
        hn_ref[...] = ((x * lax.rsqrt(ms + RMS_EPS)) * g_ref[...]).astype(BF16)

    def segment(k, ep, cw, w0):
        chunk_major = k == 0 and chunk_major_first
        for c0 in range(0, tn, cw):
            acc = jnp.dot(hn_ref[...], w_ref[:, w0 + c0:w0 + c0 + cw], preferred_element_type=F32)
            val = ep(acc, aux)
            if chunk_major:
                tmp_ref = rest[-1]
                for c in range(cw // 128):
                    tmp_ref[c0 // 128 + c] = val[:, c * 128:(c + 1) * 128]
            else:
                outs[k][:, c0:c0 + cw] = val.astype(outs[k].dtype)
        if chunk_major:
            n_blk, n_rows = tmp_ref.shape[0], tmp_ref.shape[1] // SSM_CHUNK
            for i in range(SSM_CHUNK):
                rows = [tmp_ref[c, pl.ds(i, n_rows, stride=SSM_CHUNK), :] for c in range(n_blk)]
                outs[0][i] = jnp.concatenate(rows, axis=1).astype(outs[0].dtype)

    if resident:
        normalise()
        for k, (ep, col_chunk) in enumerate(epilogues):
            segment(k, ep, col_chunk or tn, k * tn)
    else:
        pl.when(j == 0)(normalise)
        for k, (ep, col_chunk) in enumerate(epilogues):
            pl.when(j == k)(functools.partial(segment, k, ep, col_chunk or tn, 0))


def _in_proj(x2, norm_g, w_bf16, epilogues, out_dtypes, aux=(), aux_specs=(), *, tm, tn, name,
             chunk_major_first=False, resident=False):
    n_tok, d = x2.shape
    n_seg = len(epilogues)
    assert w_bf16.shape == (d, n_seg * tn) and n_tok % tm == 0
    if resident:
        w_spec = pl.BlockSpec((d, n_seg * tn), lambda i, j: (0, 0), pipeline_mode=pl.Buffered(1))
    else:
        w_spec = pl.BlockSpec((d, tn), lambda i, j: (0, j))
    out_shape = [jax.ShapeDtypeStruct((n_tok, tn), dt) for dt in out_dtypes]
    out_specs = [pl.BlockSpec((tm, tn), lambda i, j: (i, 0)) for _ in range(n_seg)]
    scratch = [pltpu.VMEM((tm, d), BF16)]
    if chunk_major_first:
        t = SSM_CHUNK
        out_shape[0] = jax.ShapeDtypeStruct((t, n_tok // t, tn), out_dtypes[0])
        out_specs[0] = pl.BlockSpec((t, tm // t, tn), lambda i, j: (0, i, 0))
        scratch.append(pltpu.VMEM((tn // 128, tm, 128), F32))
    return pl.pallas_call(
        functools.partial(_in_proj_body, epilogues=epilogues, n_aux=len(aux),
                          chunk_major_first=chunk_major_first, tn=tn, resident=resident),
        grid=(n_tok // tm, 1 if resident else n_seg),
        in_specs=[pl.BlockSpec((tm, d), lambda i, j: (i, 0)),
                  pl.BlockSpec((1, d), lambda i, j: (0, 0)),
                  w_spec] + list(aux_specs),
        out_specs=out_specs,
        out_shape=out_shape,
        scratch_shapes=scratch,
        compiler_params=_cparams("parallel", "arbitrary"),
        name=name,
    )(x2, norm_g.reshape(1, d), w_bf16, *aux)


def _ep_identity(acc, aux):
    return acc


def _ep_silu(acc, aux):
    return _silu(acc)


def _ep_gelu(acc, aux):
    return _gelu_tanh(acc)


def _ep_gelu_layernorm(acc, aux):
    ln_g, ln_b = aux[0][...], aux[1][...]
    z = _gelu_tanh(acc)
    mu = jnp.mean(z, axis=-1, keepdims=True)
    zc = z - mu
    y = zc * lax.rsqrt(jnp.mean(zc * zc, axis=-1, keepdims=True) + LN_EPS)
    return y * ln_g + ln_b


def _rope(acc, tab):
    c, s_lo, s_hi = tab[:, 0:128], tab[:, 128:256], tab[:, 256:384]
    pieces = []
    for h in range(acc.shape[1] // DA_HEAD_DIM):
        blk = acc[:, h * DA_HEAD_DIM:(h + 1) * DA_HEAD_DIM]
        rot = (blk * c + pltpu.roll(blk, ROT_DIM // 2, 1) * s_lo
               + pltpu.roll(blk, DA_HEAD_DIM - ROT_DIM // 2, 1) * s_hi)
        pieces.append(rot)
    return jnp.concatenate(pieces, axis=1)


def _ep_rope_scaled(acc, aux):
    return _rope(acc, aux[0][...]) * (DA_HEAD_DIM ** -0.5 * math.log2(math.e))


def _ep_rope(acc, aux):
    return _rope(acc, aux[0][...])


def _rope_table(seq):
    pos = jnp.arange(seq, dtype=F32)
    inv_freq = ROPE_THETA ** (-jnp.arange(0, ROT_DIM, 2, dtype=F32) / ROT_DIM)
    ang = pos[:, None] * inv_freq[None, :]
    cos, sin = jnp.cos(ang), jnp.sin(ang)
    half = ROT_DIM // 2
    pad = DA_HEAD_DIM - ROT_DIM
    c = jnp.concatenate([cos, cos, jnp.ones((seq, pad), F32)], axis=1)
    s_lo = jnp.concatenate([jnp.zeros((seq, half), F32), sin, jnp.zeros((seq, pad), F32)], axis=1)
    s_hi = jnp.concatenate([-sin, jnp.zeros((seq, half + pad), F32)], axis=1)
    return jnp.concatenate([c, s_lo, s_hi], axis=1)


def _s5_prep_body(lr_ref, li_ref, ldt_ref, bre_ref, bim_ref, bbre_ref, bbim_ref, a_ref, pw_ref, *, n_pow):
    lr, li = lr_ref[...], li_ref[...]
    dt = jnp.exp(ldt_ref[...])
    mag = jnp.exp(lr * dt)
    a_re = mag * jnp.cos(li * dt)
    a_im = mag * jnp.sin(li * dt)
    den = lr * lr + li * li
    nr = a_re - 1.0
    f_re = (nr * lr + a_im * li) / den
    f_im = (a_im * lr - nr * li) / den
    b_re, b_im = bre_ref[...], bim_ref[...]
    bbre_ref[...] = f_re * b_re - f_im * b_im
    bbim_ref[...] = f_re * b_im + f_im * b_re
    a_ref[0:1, :] = a_re
    a_ref[1:2, :] = a_im
    mag_t = jnp.exp(lr * dt * SSM_CHUNK)
    pr = mag_t * jnp.cos(li * dt * SSM_CHUNK)
    pi_ = mag_t * jnp.sin(li * dt * SSM_CHUNK)
    for k in range(n_pow):
        pw_ref[k:k + 1, :] = pr
        pw_ref[n_pow + k:n_pow + k + 1, :] = pi_
        pr, pi_ = pr * pr - pi_ * pi_, 2.0 * pr * pi_


def _s5_prep(lam_re, lam_im, log_dt, b_re, b_im, n_pow):
    g, p = lam_re.shape
    h = b_re.shape[2]
    row = lambda a: a.reshape(1, g * p)
    chan = lambda b: b.transpose(2, 0, 1).reshape(h, g * p)
    ins = [row(lam_re), row(lam_im), row(jnp.broadcast_to(log_dt[:, None], (g, p))), chan(b_re), chan(b_im)]
    out_shape = [jax.ShapeDtypeStruct((h, g * p), F32), jax.ShapeDtypeStruct((h, g * p), F32),
                 jax.ShapeDtypeStruct((2, g * p), F32), jax.ShapeDtypeStruct((2 * n_pow, g * p), F32)]
    return pl.pallas_call(
        functools.partial(_s5_prep_body, n_pow=n_pow),
        out_shape=out_shape,
        compiler_params=pltpu.CompilerParams(vmem_limit_bytes=VMEM_LIMIT_BYTES),
        name="s5_prep",
    )(*ins)


def _block_diag(w4):
    eye = jnp.eye(w4.shape[1], dtype=w4.dtype)
    return w4[:, :, :, None, :] * eye[None, :, None, :, None]


def _s5_operators(bb_re, bb_im, a, pw, c_re, c_im, d_skip, n_pow):
    g, h, p = c_re.shape
    gt = S5_TILE_GROUPS
    nq = g // gt
    sw = gt * p

    def b_blk(bb):
        w4 = bb.reshape(h, nq, gt, p).transpose(1, 2, 0, 3)
        return _block_diag(w4).reshape(nq, gt * h, sw)

    def c_blk(c):
        w4 = c.reshape(nq, gt, h, p)
        return _block_diag(w4).transpose(0, 3, 4, 1, 2).reshape(nq, sw, gt * h)

    bblk = jnp.concatenate([b_blk(bb_re), b_blk(bb_im)], axis=2).astype(BF16)
    cblk = jnp.concatenate([c_blk(c_re), -c_blk(c_im)], axis=1).astype(BF16)
    a_rows = a.reshape(2, nq, sw).transpose(1, 0, 2)
    pw_rows = pw.reshape(2 * n_pow, nq, sw).transpose(1, 0, 2)
    d_rows = d_skip.reshape(nq, 1, gt * h)
    return bblk, cblk, a_rows, pw_rows, d_rows


def _s5_body(x_ref, bblk_ref, cblk_ref, a_ref, pw_ref, d_ref, y_ref, h_ref, *, n_pow):
    t = x_ref.shape[0]
    n_rows = x_ref.shape[1]
    half = h_ref.shape[1] // 2
    a_re, a_im = a_ref[0, 0:1, :], a_ref[0, 1:2, :]

    def advance(i):
        bu = jnp.dot(x_ref[i], bblk_ref[0], preferred_element_type=F32)
        h_re, h_im = h_ref[:, 0:half], h_ref[:, half:]
        n_re = a_re * h_re - a_im * h_im + bu[:, 0:half]
        n_im = a_re * h_im + a_im * h_re + bu[:, half:]
        h_ref[:, 0:half] = n_re
        h_ref[:, half:] = n_im
        return n_re, n_im

    h_ref[...] = jnp.zeros(h_ref.shape, F32)

    def local_step(i, c):
        advance(i)
        return c

    lax.fori_loop(0, t, local_step, 0, unroll=2)

    s_re, s_im = h_ref[:, 0:half], h_ref[:, half:]
    cidx = lax.broadcasted_iota(jnp.int32, (n_rows, half), 0)
    for k in range(n_pow):
        d = 1 << k
        p_re = jnp.where(cidx >= d, pltpu.roll(s_re, d, 0), 0.0)
        p_im = jnp.where(cidx >= d, pltpu.roll(s_im, d, 0), 0.0)
        w_re, w_im = pw_ref[0, k:k + 1, :], pw_ref[0, n_pow + k:n_pow + k + 1, :]
        s_re, s_im = s_re + w_re * p_re - w_im * p_im, s_im + w_re * p_im + w_im * p_re
    h_ref[:, 0:half] = jnp.where(cidx >= 1, pltpu.roll(s_re, 1, 0), 0.0)
    h_ref[:, half:] = jnp.where(cidx >= 1, pltpu.roll(s_im, 1, 0), 0.0)

    def emit_step(i, c):
        n_re, n_im = advance(i)
        hb = jnp.concatenate([n_re, n_im], axis=1).astype(BF16)
        y = jnp.dot(hb, cblk_ref[0], preferred_element_type=F32)
        y_ref[i] = y + d_ref[0] * x_ref[i].astype(F32)
        return c

    lax.fori_loop(0, t, emit_step, 0, unroll=2)


def _s5(xa_cm, bblk, cblk, a_rows, pw_rows, d_rows, bsz, n_pow):
    t, n_rows, w = xa_cm.shape
    nq, tile, s2 = bblk.shape
    rows_b = n_rows // bsz
    per_q = lambda a: pl.BlockSpec((1,) + a.shape[1:], lambda q, b: (q, 0, 0))
    act = pl.BlockSpec((t, rows_b, tile), lambda q, b: (0, b, q))
    return pl.pallas_call(
        functools.partial(_s5_body, n_pow=n_pow),
        grid=(nq, bsz),
        in_specs=[act, per_q(bblk), per_q(cblk), per_q(a_rows), per_q(pw_rows), per_q(d_rows)],
        out_specs=act,
        out_shape=jax.ShapeDtypeStruct((t, n_rows, w), F32),
        scratch_shapes=[pltpu.VMEM((rows_b, s2), F32)],
        compiler_params=_cparams("parallel", "parallel"),
        name="s5_scan",
    )(xa_cm, bblk, cblk, a_rows, pw_rows, d_rows)


def _even_out_body(y_ref, ga_ref, u_ref, v_ref, gb_ref, x_ref, wglu_ref, bglu_ref, wsp_ref, bsp_ref,
                   wout_ref, o_ref, mix_ref, ytok_ref, *, tm):
    n_blk = ytok_ref.shape[0]
    w = n_blk * 128
    for i in range(SSM_CHUNK):
        for c in range(n_blk):
            ytok_ref[c, pl.ds(i, tm // SSM_CHUNK, stride=SSM_CHUNK), :] = y_ref[i, :, c * 128:(c + 1) * 128]
    ya = _gelu_tanh(jnp.concatenate([ytok_ref[c] for c in range(n_blk)], axis=1))
    z = jnp.dot(ya.astype(BF16), wglu_ref[...], preferred_element_type=F32) + bglu_ref[...]
    ya = ya * _sigmoid(z)
    ya = ya * ga_ref[...].astype(F32)
    mix_ref[:, 0:w] = ya.astype(BF16)

    tri = (lax.broadcasted_iota(jnp.int32, (SG_CHUNK, SG_CHUNK), 1)
           <= lax.broadcasted_iota(jnp.int32, (SG_CHUNK, SG_CHUNK), 0))
    hd = w // SG_HEADS
    for g in range(SG_HEADS):
        w_c = jnp.where(tri, wsp_ref[g], 0.0).astype(BF16)
        bias = bsp_ref[:, g:g + 1]
        for n in range(tm // SG_CHUNK):
            rs = slice(n * SG_CHUNK, (n + 1) * SG_CHUNK)
            cs = slice(g * hd, (g + 1) * hd)
            s = jnp.dot(w_c, v_ref[rs, cs], preferred_element_type=F32) + bias
            yb = u_ref[rs, cs].astype(F32) * s * gb_ref[rs, cs].astype(F32)
            mix_ref[rs, w + g * hd:w + (g + 1) * hd] = yb.astype(BF16)

    o_ref[...] = x_ref[...] + jnp.dot(mix_ref[...], wout_ref[...], preferred_element_type=F32)


def _even_out(y, ga, u, v, gb, x2, w_glu, b_glu, w_sp, b_sp_t, w_out, *, tm):
    n_tok, d = x2.shape
    t, _, w = y.shape
    tok = lambda width: pl.BlockSpec((tm, width), lambda i: (i, 0))
    full = lambda a: pl.BlockSpec(a.shape, lambda i: (0,) * a.ndim)
    return pl.pallas_call(
        functools.partial(_even_out_body, tm=tm),
        grid=(n_tok // tm,),
        in_specs=[pl.BlockSpec((t, tm // t, w), lambda i: (0, i, 0)), tok(w), tok(w), tok(w), tok(w), tok(d),
                  full(w_glu), full(b_glu), full(w_sp), full(b_sp_t), full(w_out)],
        out_specs=tok(d),
        out_shape=jax.ShapeDtypeStruct((n_tok, d), F32),
        scratch_shapes=[pltpu.VMEM((tm, 2 * w), BF16), pltpu.VMEM((w // 128, tm, 128), F32)],
        compiler_params=_cparams("parallel"),
        name="even_out",
    )(y, ga, u, v, gb, x2, w_glu, b_glu, w_sp, b_sp_t, w_out)


def _attn_body(lq1_ref, lk1_ref, lq2_ref, lk2_ref, sg_ref, q_ref, k_ref, v_ref, o_ref,
               s_ref, mx_ref, mrep_ref, ls_ref, acc_ref, *, tq, lambda_init):
    qi = pl.program_id(2)
    hd = DA_HEAD_DIM
    n_lane = tq // 128
    lam = (jnp.exp(jnp.sum(lq1_ref[...] * lk1_ref[...])) - jnp.exp(jnp.sum(lq2_ref[...] * lk2_ref[...]))
           + lambda_init)
    q = q_ref[0]
    qs = (q[:, 0:hd], q[:, hd:2 * hd])
    mx_ref[...] = jnp.full(mx_ref.shape, NEG_BIG, F32)
    ls_ref[...] = jnp.zeros(ls_ref.shape, F32)
    acc_ref[...] = jnp.zeros(acc_ref.shape, F32)

    def lane_fold(x, op):
        out = x[:, 0:128]
        for c in range(1, n_lane):
            out = op(out, x[:, c * 128:(c + 1) * 128])
        return out

    def scores(j, masked):
        k0 = pl.multiple_of(j * tq, tq)
        kj = k_ref[0, pl.ds(k0, tq), :]
        for m in range(2):
            s = lax.dot_general(qs[m], kj[:, m * hd:(m + 1) * hd], (((1,), (1,)), ((), ())),
                                preferred_element_type=F32)
            if masked:
                keep = (lax.broadcasted_iota(jnp.int32, (tq, tq), 1)
                        <= lax.broadcasted_iota(jnp.int32, (tq, tq), 0))
                s = jnp.where(keep, s, NEG_BIG)
            s_ref[m, j] = s
            mx_ref[m] = jnp.maximum(mx_ref[m], lane_fold(s, jnp.maximum))

    def scores_pair(jp, c):
        scores(2 * jp, False)
        scores(2 * jp + 1, False)
        return c

    lax.fori_loop(0, qi // 2, scores_pair, 0)

    @pl.when(qi % 2 == 1)
    def _():
        scores(qi - 1, False)

    scores(qi, True)

    for m in range(2):
        mrep_ref[m] = jnp.broadcast_to(jnp.max(mx_ref[m], axis=-1, keepdims=True), (tq, 128))

    def accumulate(j):
        k0 = pl.multiple_of(j * tq, tq)
        vj = v_ref[0, pl.ds(k0, tq), :]
        for m in range(2):
            s = s_ref[m, j]
            mrep = mrep_ref[m]
            p = jnp.concatenate([jnp.exp2(s[:, c_ * 128:(c_ + 1) * 128] - mrep) for c_ in range(n_lane)],
                                axis=1)
            ls_ref[m] += lane_fold(p, jnp.add)
            acc_ref[m] += jnp.dot(p.astype(BF16), vj, preferred_element_type=F32)

    def accumulate_pair(jp, c):
        accumulate(2 * jp)
        accumulate(2 * jp + 1)
        return c

    lax.fori_loop(0, (qi + 1) // 2, accumulate_pair, 0)

    @pl.when(qi % 2 == 0)
    def _():
        accumulate(qi)

    o1 = acc_ref[0] / jnp.sum(ls_ref[0], axis=-1, keepdims=True)
    o2 = acc_ref[1] / jnp.sum(ls_ref[1], axis=-1, keepdims=True)
    o = o1 - lam * o2
    ms = jnp.mean(o * o, axis=-1, keepdims=True)
    o = (o * lax.rsqrt(ms + RMS_EPS)) * sg_ref[...]
    o_ref[0] = (o * (1.0 - lambda_init)).astype(o_ref.dtype)


def _diff_attn(q, k, v, lq1, lk1, lq2, lk2, subln_g, lambda_init, *, tq):
    b, seq, width = q.shape
    vec = lambda a: a.reshape(1, -1)
    small = lambda n: pl.BlockSpec((1, n), lambda bi, hi, qi: (0, 0))
    scratch = [pltpu.VMEM((2, seq // tq, tq, tq), F32),
               pltpu.VMEM((2, tq, 128), F32),
               pltpu.VMEM((2, tq, 128), F32),
               pltpu.VMEM((2, tq, 128), F32),
               pltpu.VMEM((2, tq, DA_V_DIM), F32)]
    return pl.pallas_call(
        functools.partial(_attn_body, tq=tq, lambda_init=lambda_init),
        grid=(b, DA_HEADS, seq // tq),
        in_specs=[small(DA_HEAD_DIM)] * 4 + [small(DA_V_DIM)] + [
            pl.BlockSpec((1, tq, DA_V_DIM), lambda bi, hi, qi: (bi, qi, hi)),
            pl.BlockSpec((1, seq, DA_V_DIM), lambda bi, hi, qi: (bi, 0, hi)),
            pl.BlockSpec((1, seq, DA_V_DIM), lambda bi, hi, qi: (bi, 0, hi))],
        out_specs=pl.BlockSpec((1, tq, DA_V_DIM), lambda bi, hi, qi: (bi, qi, hi)),
        out_shape=jax.ShapeDtypeStruct((b, seq, width), BF16),
        scratch_shapes=scratch,
        compiler_params=_cparams("parallel", "parallel", "arbitrary"),
        name="diff_attn",
    )(vec(lq1), vec(lk1), vec(lq2), vec(lk2), vec(subln_g), q, k, v)


def _odd_out_body(o_ref, g_ref, x_ref, w_ref, fn_ref, out_ref, *, final_norm):
    gated = (o_ref[...].astype(F32) * g_ref[...].astype(F32)).astype(BF16)
    y = x_ref[...] + jnp.dot(gated, w_ref[...], preferred_element_type=F32)
    if final_norm:
        ms = jnp.mean(y * y, axis=-1, keepdims=True)
        y = (y * lax.rsqrt(ms + RMS_EPS)) * fn_ref[...]
    out_ref[...] = y


def _odd_out(o, g, x2, w_out, final_g, *, tm, final_norm):
    n_tok, d = x2.shape
    tok = lambda dt: pl.BlockSpec((tm, d), lambda i: (i, 0))
    return pl.pallas_call(
        functools.partial(_odd_out_body, final_norm=final_norm),
        grid=(n_tok // tm,),
        in_specs=[tok(BF16), tok(BF16), tok(F32),
                  pl.BlockSpec(w_out.shape, lambda i: (0, 0)),
                  pl.BlockSpec((1, d), lambda i: (0, 0))],
        out_specs=tok(F32),
        out_shape=jax.ShapeDtypeStruct((n_tok, d), F32),
        compiler_params=_cparams("parallel"),
        name="odd_out",
    )(o, g, x2, w_out, final_g.reshape(1, d))


def _even_layer(x2, bsz, seq, norm_g, w_in, lam_re, lam_im, log_dt, b_re, b_im, c_re, c_im, d_skip,
                w_glu, b_glu, ln_g, ln_b, w_sp, b_sp, w_out):
    n_tok, d = x2.shape
    w = d // 2
    n_chunks = seq // SSM_CHUNK
    n_pow = max(1, (n_chunks - 1).bit_length())

    full = lambda a: pl.BlockSpec(a.shape, lambda i, j: (0, 0))
    ln_g2, ln_b2 = ln_g.reshape(1, w), ln_b.reshape(1, w)
    xa_cm, ga, u, v, gb = _in_proj(
        x2, norm_g, w_in.astype(BF16),
        [(_ep_identity, IN_PROJ_COL_CHUNK), (_ep_silu, IN_PROJ_COL_CHUNK), (_ep_gelu, IN_PROJ_COL_CHUNK),
         (_ep_gelu_layernorm, None), (_ep_silu, IN_PROJ_COL_CHUNK)],
        [BF16] * 5, aux=(ln_g2, ln_b2), aux_specs=(full(ln_g2), full(ln_b2)),
        tm=512, tn=w, name="even_in_proj", chunk_major_first=True, resident=True)

    bb_re, bb_im, a, pw = _s5_prep(lam_re, lam_im, log_dt, b_re, b_im, n_pow)
    ops = _s5_operators(bb_re, bb_im, a, pw, c_re, c_im, d_skip, n_pow)
    y = _s5(xa_cm, *ops, bsz, n_pow)

    return _even_out(y, ga, u, v, gb, x2, w_glu.astype(BF16), b_glu.reshape(1, w),
                     w_sp, b_sp.T, w_out.astype(BF16), tm=512)


def _odd_layer(x2, bsz, seq, norm_g, w_in, lq1, lk1, lq2, lk2, subln_g, w_out, lambda_init,
               final_g, final_norm):
    n_tok, d = x2.shape
    tm = 512
    tab = _rope_table(seq)
    n_pos_blocks = seq // tm
    tab_spec = pl.BlockSpec((tm, tab.shape[1]), lambda i, j: (i % n_pos_blocks, 0))
    q, k, v, g = _in_proj(
        x2, norm_g, w_in.astype(BF16),
        [(_ep_rope_scaled, IN_PROJ_COL_CHUNK), (_ep_rope, IN_PROJ_COL_CHUNK),
         (_ep_identity, IN_PROJ_COL_CHUNK), (_ep_silu, IN_PROJ_COL_CHUNK)],
        [BF16] * 4, aux=(tab,), aux_specs=(tab_spec,), tm=tm, tn=d, name="odd_in_proj")
    shp = (bsz, seq, d)
    o = _diff_attn(q.reshape(shp), k.reshape(shp), v.reshape(shp), lq1, lk1, lq2, lk2, subln_g,
                   lambda_init, tq=ATTN_TILE)
    return _odd_out(o.reshape(n_tok, d), g, x2, w_out.astype(BF16), final_g, tm=512,
                    final_norm=final_norm)


def kernel(x, ev_norm, ev_w_in, ssm_lam_re, ssm_lam_im, ssm_log_dt, ssm_b_re, ssm_b_im, ssm_c_re, ssm_c_im, ssm_d, ssm_w_glu, ssm_b_glu, sg_ln_g, sg_ln_b, sg_w_sp, sg_b_sp, ev_w_out, od_norm, od_w_in, da_lq1, da_lk1, da_lq2, da_lk2, da_subln, od_w_out, final_norm):
    bsz, seq, d = x.shape
    depth = ev_norm.shape[0] + od_norm.shape[0]
    assert depth % 2 == 0, "the final rmsnorm is fused into the last (odd) layer"
    x2 = x.reshape(bsz * seq, d)
    for i in range(depth):
        j = i // 2
        if i % 2 == 0:
            x2 = _even_layer(x2, bsz, seq, ev_norm[j], ev_w_in[j], ssm_lam_re[j], ssm_lam_im[j],
                             ssm_log_dt[j], ssm_b_re[j], ssm_b_im[j], ssm_c_re[j], ssm_c_im[j],
                             ssm_d[j], ssm_w_glu[j], ssm_b_glu[j], sg_ln_g[j], sg_ln_b[j],
                             sg_w_sp[j], sg_b_sp[j], ev_w_out[j])
        else:
            lambda_init = 0.8 - 0.6 * math.exp(-0.3 * i)
            x2 = _odd_layer(x2, bsz, seq, od_norm[j], od_w_in[j], da_lq1[j], da_lk1[j], da_lq2[j],
                            da_lk2[j], da_subln[j], od_w_out[j], lambda_init, final_norm,
                            final_norm=(i == depth - 1))
    return x2.reshape(bsz, seq, d)
```

```python
import functools
import math

import jax
import jax.numpy as jnp
from jax import lax
from jax.experimental import pallas as pl
from jax.experimental.pallas import tpu as pltpu

F32 = jnp.float32
BF16 = jnp.bfloat16

RMS_EPS = 1e-6
LN_EPS = 1e-5

SSM_GROUP = 16
SSM_STATE = 64
SSM_CHUNK = 16
S5_TILE_GROUPS = 16
SG_CHUNK = 128
SG_HEADS = 8
DA_HEADS = 8
DA_HEAD_DIM = 128
DA_V_DIM = 256
ROT_DIM = 32
ROPE_THETA = 500000.0
NEG_BIG = -1e30
ATTN_TILE = 512
IN_PROJ_COL_CHUNK = 512

VMEM_LIMIT_BYTES = 56 * 1024 * 1024


def _cparams(*sem):
    return pltpu.CompilerParams(dimension_semantics=sem, vmem_limit_bytes=VMEM_LIMIT_BYTES)


def _gelu_tanh(x):
    return 0.5 * x * (1.0 + jnp.tanh(math.sqrt(2.0 / math.pi) * (x + 0.044715 * (x * x * x))))


def _sigmoid(x):
    return 1.0 / (1.0 + jnp.exp(-x))


def _silu(x):
    return x * _sigmoid(x)


def _in_proj_body(x_ref, g_ref, w_ref, *rest, epilogues, n_aux, chunk_major_first, tn, resident):
    aux = rest[:n_aux]
    outs = rest[n_aux:n_aux + len(epilogues)]
    hn_ref = rest[n_aux + len(epilogues)]
    j = pl.program_id(1)

    def normalise():
        x = x_ref[...]
        ms = jnp.mean(x * x, axis=-1, keepdims=True)
        hn_ref[...] = ((x * lax.rsqrt(ms + RMS_EPS)) * g_ref[...]).astype(BF16)

    def segment(k, ep, cw, w0):
        chunk_major = k == 0 and chunk_major_first
        for c0 in range(0, tn, cw):
            acc = jnp.dot(hn_ref[...], w_ref[:, w0 + c0:w0 + c0 + cw], preferred_element_type=F32)
            val = ep(acc, aux)
            if chunk_major:
                tmp_ref = rest[-1]
                for c in range(cw // 128):
                    tmp_ref[c0 // 128 + c] = val[:, c * 128:(c + 1) * 128]
            else:
                outs[k][:, c0:c0 + cw] = val.astype(outs[k].dtype)
        if chunk_major:
            n_blk, n_rows = tmp_ref.shape[0], tmp_ref.shape[1] // SSM_CHUNK
            for i in range(SSM_CHUNK):
                rows = [tmp_ref[c, pl.ds(i, n_rows, stride=SSM_CHUNK), :] for c in range(n_blk)]
                outs[0][i] = jnp.concatenate(rows, axis=1).astype(outs[0].dtype)

    if resident:
        normalise()
        for k, (ep, col_chunk) in enumerate(epilogues):
            segment(k, ep, col_chunk or tn, k * tn)
    else:
        pl.when(j == 0)(normalise)
        for k, (ep, col_chunk) in enumerate(epilogues):
            pl.when(j == k)(functools.partial(segment, k, ep, col_chunk or tn, 0))


def _in_proj(x2, norm_g, w_bf16, epilogues, out_dtypes, aux=(), aux_specs=(), *, tm, tn, name,
             chunk_major_first=False, resident=False):
    n_tok, d = x2.shape
    n_seg = len(epilogues)
    assert w_bf16.shape == (d, n_seg * tn) and n_tok % tm == 0
    if resident:
        w_spec = pl.BlockSpec((d, n_seg * tn), lambda i, j: (0, 0), pipeline_mode=pl.Buffered(1))
    else:
        w_spec = pl.BlockSpec((d, tn), lambda i, j: (0, j))
    out_shape = [jax.ShapeDtypeStruct((n_tok, tn), dt) for dt in out_dtypes]
    out_specs = [pl.BlockSpec((tm, tn), lambda i, j: (i, 0)) for _ in range(n_seg)]
    scratch = [pltpu.VMEM((tm, d), BF16)]
    if chunk_major_first:
        t = SSM_CHUNK
        out_shape[0] = jax.ShapeDtypeStruct((t, n_tok // t, tn), out_dtypes[0])
        out_specs[0] = pl.BlockSpec((t, tm // t, tn), lambda i, j: (0, i, 0))
        scratch.append(pltpu.VMEM((tn // 128, tm, 128), F32))
    return pl.pallas_call(
        functools.partial(_in_proj_body, epilogues=epilogues, n_aux=len(aux),
                          chunk_major_first=chunk_major_first, tn=tn, resident=resident),
        grid=(n_tok // tm, 1 if resident else n_seg),
        in_specs=[pl.BlockSpec((tm, d), lambda i, j: (i, 0)),
                  pl.BlockSpec((1, d), lambda i, j: (0, 0)),
                  w_spec] + list(aux_specs),
        out_specs=out_specs,
        out_shape=out_shape,
        scratch_shapes=scratch,
        compiler_params=_cparams("parallel", "arbitrary"),
        name=name,
    )(x2, norm_g.reshape(1, d), w_bf16, *aux)


def _ep_identity(acc, aux):
    return acc


def _ep_silu(acc, aux):
    return _silu(acc)


def _ep_gelu(acc, aux):
    return _gelu_tanh(acc)


def _ep_gelu_layernorm(acc, aux):
    ln_g, ln_b = aux[0][...], aux[1][...]
    z = _gelu_tanh(acc)
    mu = jnp.mean(z, axis=-1, keepdims=True)
    zc = z - mu
    y = zc * lax.rsqrt(jnp.mean(zc * zc, axis=-1, keepdims=True) + LN_EPS)
    return y * ln_g + ln_b


def _rope(acc, tab):
    c, s_lo, s_hi = tab[:, 0:128], tab[:, 128:256], tab[:, 256:384]
    pieces = []
    for h in range(acc.shape[1] // DA_HEAD_DIM):
        blk = acc[:, h * DA_HEAD_DIM:(h + 1) * DA_HEAD_DIM]
        rot = (blk * c + pltpu.roll(blk, ROT_DIM // 2, 1) * s_lo
               + pltpu.roll(blk, DA_HEAD_DIM - ROT_DIM // 2, 1) * s_hi)
        pieces.append(rot)
    return jnp.concatenate(pieces, axis=1)


def _ep_rope_scaled(acc, aux):
    return _rope(acc, aux[0][...]) * (DA_HEAD_DIM ** -0.5 * math.log2(math.e))


def _ep_rope(acc, aux):
    return _rope(acc, aux[0][...])


def _rope_table(seq):
    pos = jnp.arange(seq, dtype=F32)
    inv_freq = ROPE_THETA ** (-jnp.arange(0, ROT_DIM, 2, dtype=F32) / ROT_DIM)
    ang = pos[:, None] * inv_freq[None, :]
    cos, sin = jnp.cos(ang), jnp.sin(ang)
    half = ROT_DIM // 2
    pad = DA_HEAD_DIM - ROT_DIM
    c = jnp.concatenate([cos, cos, jnp.ones((seq, pad), F32)], axis=1)
    s_lo = jnp.concatenate([jnp.zeros((seq, half), F32), sin, jnp.zeros((seq, pad), F32)], axis=1)
    s_hi = jnp.concatenate([-sin, jnp.zeros((seq, half + pad), F32)], axis=1)
    return jnp.concatenate([c, s_lo, s_hi], axis=1)


def _s5_prep_body(lr_ref, li_ref, ldt_ref, bre_ref, bim_ref, bbre_ref, bbim_ref, a_ref, pw_ref, *, n_pow):
    lr, li = lr_ref[...], li_ref[...]
    dt = jnp.exp(ldt_ref[...])
    mag = jnp.exp(lr * dt)
    a_re = mag * jnp.cos(li * dt)
    a_im = mag * jnp.sin(li * dt)
    den = lr * lr + li * li
    nr = a_re - 1.0
    f_re = (nr * lr + a_im * li) / den
    f_im = (a_im * lr - nr * li) / den
    b_re, b_im = bre_ref[...], bim_ref[...]
    bbre_ref[...] = f_re * b_re - f_im * b_im
    bbim_ref[...] = f_re * b_im + f_im * b_re
    a_ref[0:1, :] = a_re
    a_ref[1:2, :] = a_im
    mag_t = jnp.exp(lr * dt * SSM_CHUNK)
    pr = mag_t * jnp.cos(li * dt * SSM_CHUNK)
    pi_ = mag_t * jnp.sin(li * dt * SSM_CHUNK)
    for k in range(n_pow):
        pw_ref[k:k + 1, :] = pr
        pw_ref[n_pow + k:n_pow + k + 1, :] = pi_
        pr, pi_ = pr * pr - pi_ * pi_, 2.0 * pr * pi_


def _s5_prep(lam_re, lam_im, log_dt, b_re, b_im, n_pow):
    g, p = lam_re.shape
    h = b_re.shape[2]
    row = lambda a: a.reshape(1, g * p)
    chan = lambda b: b.transpose(2, 0, 1).reshape(h, g * p)
    ins = [row(lam_re), row(lam_im), row(jnp.broadcast_to(log_dt[:, None], (g, p))), chan(b_re), chan(b_im)]
    out_shape = [jax.ShapeDtypeStruct((h, g * p), F32), jax.ShapeDtypeStruct((h, g * p), F32),
                 jax.ShapeDtypeStruct((2, g * p), F32), jax.ShapeDtypeStruct((2 * n_pow, g * p), F32)]
    return pl.pallas_call(
        functools.partial(_s5_prep_body, n_pow=n_pow),
        out_shape=out_shape,
        compiler_params=pltpu.CompilerParams(vmem_limit_bytes=VMEM_LIMIT_BYTES),
        name="s5_prep",
    )(*ins)


def _block_diag(w4):
    eye = jnp.eye(w4.shape[1], dtype=w4.dtype)
    return w4[:, :, :, None, :] * eye[None, :, None, :, None]


def _s5_operators(bb_re, bb_im, a, pw, c_re, c_im, d_skip, n_pow):
    g, h, p = c_re.shape
    gt = S5_TILE_GROUPS
    nq = g // gt
    sw = gt * p

    def b_blk(bb):
        w4 = bb.reshape(h, nq, gt, p).transpose(1, 2, 0, 3)
        return _block_diag(w4).reshape(nq, gt * h, sw)

    def c_blk(c):
        w4 = c.reshape(nq, gt, h, p)
        return _block_diag(w4).transpose(0, 3, 4, 1, 2).reshape(nq, sw, gt * h)

    bblk = jnp.concatenate([b_blk(bb_re), b_blk(bb_im)], axis=2).astype(BF16)
    cblk = jnp.concatenate([c_blk(c_re), -c_blk(c_im)], axis=1).astype(BF16)
    a_rows = a.reshape(2, nq, sw).transpose(1, 0, 2)
    pw_rows = pw.reshape(2 * n_pow, nq, sw).transpose(1, 0, 2)
    d_rows = d_skip.reshape(nq, 1, gt * h)
    return bblk, cblk, a_rows, pw_rows, d_rows


def _s5_body(x_ref, bblk_ref, cblk_ref, a_ref, pw_ref, d_ref, y_ref, h_ref, *, n_pow):
    t = x_ref.shape[0]
    n_rows = x_ref.shape[1]
    half = h_ref.shape[1] // 2
    a_re, a_im = a_ref[0, 0:1, :], a_ref[0, 1:2, :]

    def advance(i):
        bu = jnp.dot(x_ref[i], bblk_ref[0], preferred_element_type=F32)
        h_re, h_im = h_ref[:, 0:half], h_ref[:, half:]
        n_re = a_re * h_re - a_im * h_im + bu[:, 0:half]
        n_im = a_re * h_im + a_im * h_re + bu[:, half:]
        h_ref[:, 0:half] = n_re
        h_ref[:, half:] = n_im
        return n_re, n_im

    h_ref[...] = jnp.zeros(h_ref.shape, F32)

    def local_step(i, c):
        advance(i)
        return c

    lax.fori_loop(0, t, local_step, 0, unroll=2)

    s_re, s_im = h_ref[:, 0:half], h_ref[:, half:]
    cidx = lax.broadcasted_iota(jnp.int32, (n_rows, half), 0)
    for k in range(n_pow):
        d = 1 << k
        p_re = jnp.where(cidx >= d, pltpu.roll(s_re, d, 0), 0.0)
        p_im = jnp.where(cidx >= d, pltpu.roll(s_im, d, 0), 0.0)
        w_re, w_im = pw_ref[0, k:k + 1, :], pw_ref[0, n_pow + k:n_pow + k + 1, :]
        s_re, s_im = s_re + w_re * p_re - w_im * p_im, s_im + w_re * p_im + w_im * p_re
    h_ref[:, 0:half] = jnp.where(cidx >= 1, pltpu.roll(s_re, 1, 0), 0.0)
    h_ref[:, half:] = jnp.where(cidx >= 1, pltpu.roll(s_im, 1, 0), 0.0)

    def emit_step(i, c):
        n_re, n_im = advance(i)
        hb = jnp.concatenate([n_re, n_im], axis=1).astype(BF16)
        y = jnp.dot(hb, cblk_ref[0], preferred_element_type=F32)
        y_ref[i] = y + d_ref[0] * x_ref[i].astype(F32)
        return c

    lax.fori_loop(0, t, emit_step, 0, unroll=2)


def _s5(xa_cm, bblk, cblk, a_rows, pw_rows, d_rows, bsz, n_pow):
    t, n_rows, w = xa_cm.shape
    nq, tile, s2 = bblk.shape
    rows_b = n_rows // bsz
    per_q = lambda a: pl.BlockSpec((1,) + a.shape[1:], lambda q, b: (q, 0, 0))
    act = pl.BlockSpec((t, rows_b, tile), lambda q, b: (0, b, q))
    return pl.pallas_call(
        functools.partial(_s5_body, n_pow=n_pow),
        grid=(nq, bsz),
        in_specs=[act, per_q(bblk), per_q(cblk), per_q(a_rows), per_q(pw_rows), per_q(d_rows)],
        out_specs=act,
        out_shape=jax.ShapeDtypeStruct((t, n_rows, w), F32),
        scratch_shapes=[pltpu.VMEM((rows_b, s2), F32)],
        compiler_params=_cparams("parallel", "parallel"),
        name="s5_scan",
    )(xa_cm, bblk, cblk, a_rows, pw_rows, d_rows)


def _even_out_body(y_ref, ga_ref, u_ref, v_ref, gb_ref, x_ref, wglu_ref, bglu_ref, wsp_ref, bsp_ref,
                   wout_ref, o_ref, mix_ref, ytok_ref, *, tm):
    n_blk = ytok_ref.shape[0]
    w = n_blk * 128
    for i in range(SSM_CHUNK):
        for c in range(n_blk):
            ytok_ref[c, pl.ds(i, tm // SSM_CHUNK, stride=SSM_CHUNK), :] = y_ref[i, :, c * 128:(c + 1) * 128]
    ya = _gelu_tanh(jnp.concatenate([ytok_ref[c] for c in range(n_blk)], axis=1))
    z = jnp.dot(ya.astype(BF16), wglu_ref[...], preferred_element_type=F32) + bglu_ref[...]
    ya = ya * _sigmoid(z)
    ya = ya * ga_ref[...].astype(F32)
    mix_ref[:, 0:w] = ya.astype(BF16)

    tri = (lax.broadcasted_iota(jnp.int32, (SG_CHUNK, SG_CHUNK), 1)
           <= lax.broadcasted_iota(jnp.int32, (SG_CHUNK, SG_CHUNK), 0))
    hd = w // SG_HEADS
    for g in range(SG_HEADS):
        w_c = jnp.where(tri, wsp_ref[g], 0.0).astype(BF16)
        bias = bsp_ref[:, g:g + 1]
        for n in range(tm // SG_CHUNK):
            rs = slice(n * SG_CHUNK, (n + 1) * SG_CHUNK)
            cs = slice(g * hd, (g + 1) * hd)
            s = jnp.dot(w_c, v_ref[rs, cs], preferred_element_type=F32) + bias
            yb = u_ref[rs, cs].astype(F32) * s * gb_ref[rs, cs].astype(F32)
            mix_ref[rs, w + g * hd:w + (g + 1) * hd] = yb.astype(BF16)

    o_ref[...] = x_ref[...] + jnp.dot(mix_ref[...], wout_ref[...], preferred_element_type=F32)


def _even_out(y, ga, u, v, gb, x2, w_glu, b_glu, w_sp, b_sp_t, w_out, *, tm):
    n_tok, d = x2.shape
    t, _, w = y.shape
    tok = lambda width: pl.BlockSpec((tm, width), lambda i: (i, 0))
    full = lambda a: pl.BlockSpec(a.shape, lambda i: (0,) * a.ndim)
    return pl.pallas_call(
        functools.partial(_even_out_body, tm=tm),
        grid=(n_tok // tm,),
        in_specs=[pl.BlockSpec((t, tm // t, w), lambda i: (0, i, 0)), tok(w), tok(w), tok(w), tok(w), tok(d),
                  full(w_glu), full(b_glu), full(w_sp), full(b_sp_t), full(w_out)],
        out_specs=tok(d),
        out_shape=jax.ShapeDtypeStruct((n_tok, d), F32),
        scratch_shapes=[pltpu.VMEM((tm, 2 * w), BF16), pltpu.VMEM((w // 128, tm, 128), F32)],
        compiler_params=_cparams("parallel"),
        name="even_out",
    )(y, ga, u, v, gb, x2, w_glu, b_glu, w_sp, b_sp_t, w_out)


def _attn_body(lq1_ref, lk1_ref, lq2_ref, lk2_ref, sg_ref, q_ref, k_ref, v_ref, o_ref, *scratch,
               tq, lambda_init):
    lam = (jnp.exp(jnp.sum(lq1_ref[...] * lk1_ref[...])) - jnp.exp(jnp.sum(lq2_ref[...] * lk2_ref[...]))
           + lambda_init)

    def q_tile(qi, c):
        _attn_tile(qi, lam, sg_ref, q_ref, k_ref, v_ref, o_ref, *scratch, tq=tq, lambda_init=lambda_init)
        return c

    lax.fori_loop(0, q_ref.shape[1] // tq, q_tile, 0)


def _attn_tile(qi, lam, sg_ref, q_ref, k_ref, v_ref, o_ref, s_ref, mx_ref, mrep_ref, ls_ref, acc_ref, *,
               tq, lambda_init):
    hd = DA_HEAD_DIM
    n_lane = tq // 128
    q0 = pl.multiple_of(qi * tq, tq)
    q = q_ref[0, pl.ds(q0, tq), :]
    qs = (q[:, 0:hd], q[:, hd:2 * hd])
    mx_ref[...] = jnp.full(mx_ref.shape, NEG_BIG, F32)
    ls_ref[...] = jnp.zeros(ls_ref.shape, F32)
    acc_ref[...] = jnp.zeros(acc_ref.shape, F32)

    def lane_fold(x, op):
        out = x[:, 0:128]
        for c in range(1, n_lane):
            out = op(out, x[:, c * 128:(c + 1) * 128])
        return out

    def scores(j, masked):
        k0 = pl.multiple_of(j * tq, tq)
        kj = k_ref[0, pl.ds(k0, tq), :]
        for m in range(2):
            s = lax.dot_general(qs[m], kj[:, m * hd:(m + 1) * hd], (((1,), (1,)), ((), ())),
                                preferred_element_type=F32)
            if masked:
                keep = (lax.broadcasted_iota(jnp.int32, (tq, tq), 1)
                        <= lax.broadcasted_iota(jnp.int32, (tq, tq), 0))
                s = jnp.where(keep, s, NEG_BIG)
            s_ref[m, j] = s
            mx_ref[m] = jnp.maximum(mx_ref[m], lane_fold(s, jnp.maximum))

    def scores_pair(jp, c):
        scores(2 * jp, False)
        scores(2 * jp + 1, False)
        return c

    lax.fori_loop(0, qi // 2, scores_pair, 0)

    @pl.when(qi % 2 == 1)
    def _():
        scores(qi - 1, False)

    scores(qi, True)

    for m in range(2):
        mrep_ref[m] = jnp.broadcast_to(jnp.max(mx_ref[m], axis=-1, keepdims=True), (tq, 128))

    def accumulate(j):
        k0 = pl.multiple_of(j * tq, tq)
        vj = v_ref[0, pl.ds(k0, tq), :]
        for m in range(2):
            s = s_ref[m, j]
            mrep = mrep_ref[m]
            p = jnp.concatenate([jnp.exp2(s[:, c_ * 128:(c_ + 1) * 128] - mrep) for c_ in range(n_lane)],
                                axis=1)
            ls_ref[m] += lane_fold(p, jnp.add)
            acc_ref[m] += jnp.dot(p.astype(BF16), vj, preferred_element_type=F32)

    def accumulate_pair(jp, c):
        accumulate(2 * jp)
        accumulate(2 * jp + 1)
        return c

    lax.fori_loop(0, (qi + 1) // 2, accumulate_pair, 0)

    @pl.when(qi % 2 == 0)
    def _():
        accumulate(qi)

    o1 = acc_ref[0] / jnp.sum(ls_ref[0], axis=-1, keepdims=True)
    o2 = acc_ref[1] / jnp.sum(ls_ref[1], axis=-1, keepdims=True)
    o = o1 - lam * o2
    ms = jnp.mean(o * o, axis=-1, keepdims=True)
    o = (o * lax.rsqrt(ms + RMS_EPS)) * sg_ref[...]
    o_ref[0, pl.ds(q0, tq), :] = (o * (1.0 - lambda_init)).astype(o_ref.dtype)


def _diff_attn(q, k, v, lq1, lk1, lq2, lk2, subln_g, lambda_init, *, tq):
    b, seq, width = q.shape
    vec = lambda a: a.reshape(1, -1)
    small = lambda n: pl.BlockSpec((1, n), lambda bi, hi: (0, 0))
    head = pl.BlockSpec((1, seq, DA_V_DIM), lambda bi, hi: (bi, 0, hi))
    scratch = [pltpu.VMEM((2, seq // tq, tq, tq), F32),
               pltpu.VMEM((2, tq, 128), F32),
               pltpu.VMEM((2, tq, 128), F32),
               pltpu.VMEM((2, tq, 128), F32),
               pltpu.VMEM((2, tq, DA_V_DIM), F32)]
    return pl.pallas_call(
        functools.partial(_attn_body, tq=tq, lambda_init=lambda_init),
        grid=(b, DA_HEADS),
        in_specs=[small(DA_HEAD_DIM)] * 4 + [small(DA_V_DIM), head, head, head],
        out_specs=head,
        out_shape=jax.ShapeDtypeStruct((b, seq, width), BF16),
        scratch_shapes=scratch,
        compiler_params=_cparams("parallel", "parallel"),
        name="diff_attn",
    )(vec(lq1), vec(lk1), vec(lq2), vec(lk2), vec(subln_g), q, k, v)


def _odd_out_body(o_ref, g_ref, x_ref, w_ref, fn_ref, out_ref, *, final_norm):
    gated = (o_ref[...].astype(F32) * g_ref[...].astype(F32)).astype(BF16)
    y = x_ref[...] + jnp.dot(gated, w_ref[...], preferred_element_type=F32)
    if final_norm:
        ms = jnp.mean(y * y, axis=-1, keepdims=True)
        y = (y * lax.rsqrt(ms + RMS_EPS)) * fn_ref[...]
    out_ref[...] = y


def _odd_out(o, g, x2, w_out, final_g, *, tm, final_norm):
    n_tok, d = x2.shape
    tok = lambda dt: pl.BlockSpec((tm, d), lambda i: (i, 0))
    return pl.pallas_call(
        functools.partial(_odd_out_body, final_norm=final_norm),
        grid=(n_tok // tm,),
        in_specs=[tok(BF16), tok(BF16), tok(F32),
                  pl.BlockSpec(w_out.shape, lambda i: (0, 0)),
                  pl.BlockSpec((1, d), lambda i: (0, 0))],
        out_specs=tok(F32),
        out_shape=jax.ShapeDtypeStruct((n_tok, d), F32),
        compiler_params=_cparams("parallel"),
        name="odd_out",
    )(o, g, x2, w_out, final_g.reshape(1, d))


def _even_layer(x2, bsz, seq, norm_g, w_in, lam_re, lam_im, log_dt, b_re, b_im, c_re, c_im, d_skip,
                w_glu, b_glu, ln_g, ln_b, w_sp, b_sp, w_out):
    n_tok, d = x2.shape
    w = d // 2
    n_chunks = seq // SSM_CHUNK
    n_pow = max(1, (n_chunks - 1).bit_length())

    full = lambda a: pl.BlockSpec(a.shape, lambda i, j: (0, 0))
    ln_g2, ln_b2 = ln_g.reshape(1, w), ln_b.reshape(1, w)
    xa_cm, ga, u, v, gb = _in_proj(
        x2, norm_g, w_in.astype(BF16),
        [(_ep_identity, IN_PROJ_COL_CHUNK), (_ep_silu, IN_PROJ_COL_CHUNK), (_ep_gelu, IN_PROJ_COL_CHUNK),
         (_ep_gelu_layernorm, None), (_ep_silu, IN_PROJ_COL_CHUNK)],
        [BF16] * 5, aux=(ln_g2, ln_b2), aux_specs=(full(ln_g2), full(ln_b2)),
        tm=512, tn=w, name="even_in_proj", chunk_major_first=True, resident=True)

    bb_re, bb_im, a, pw = _s5_prep(lam_re, lam_im, log_dt, b_re, b_im, n_pow)
    ops = _s5_operators(bb_re, bb_im, a, pw, c_re, c_im, d_skip, n_pow)
    y = _s5(xa_cm, *ops, bsz, n_pow)

    return _even_out(y, ga, u, v, gb, x2, w_glu.astype(BF16), b_glu.reshape(1, w),
                     w_sp, b_sp.T, w_out.astype(BF16), tm=512)


def _odd_layer(x2, bsz, seq, norm_g, w_in, lq1, lk1, lq2, lk2, subln_g, w_out, lambda_init,
               final_g, final_norm):
    n_tok, d = x2.shape
    tm = 512
    tab = _rope_table(seq)
    n_pos_blocks = seq // tm
    tab_spec = pl.BlockSpec((tm, tab.shape[1]), lambda i, j: (i % n_pos_blocks, 0))
    q, k, v, g = _in_proj(
        x2, norm_g, w_in.astype(BF16),
        [(_ep_rope_scaled, IN_PROJ_COL_CHUNK), (_ep_rope, IN_PROJ_COL_CHUNK),
         (_ep_identity, IN_PROJ_COL_CHUNK), (_ep_silu, IN_PROJ_COL_CHUNK)],
        [BF16] * 4, aux=(tab,), aux_specs=(tab_spec,), tm=tm, tn=d, name="odd_in_proj")
    shp = (bsz, seq, d)
    o = _diff_attn(q.reshape(shp), k.reshape(shp), v.reshape(shp), lq1, lk1, lq2, lk2, subln_g,
                   lambda_init, tq=ATTN_TILE)
    return _odd_out(o.reshape(n_tok, d), g, x2, w_out.astype(BF16), final_g, tm=512,
                    final_norm=final_norm)


def kernel(x, ev_norm, ev_w_in, ssm_lam_re, ssm_lam_im, ssm_log_dt, ssm_b_re, ssm_b_im, ssm_c_re, ssm_c_im, ssm_d, ssm_w_glu, ssm_b_glu, sg_ln_g, sg_ln_b, sg_w_sp, sg_b_sp, ev_w_out, od_norm, od_w_in, da_lq1, da_lk1, da_lq2, da_lk2, da_subln, od_w_out, final_norm):
    bsz, seq, d = x.shape
    depth = ev_norm.shape[0] + od_norm.shape[0]
    assert depth % 2 == 0, "the final rmsnorm is fused into the last (odd) layer"
    x2 = x.reshape(bsz * seq, d)
    for i in range(depth):
        j = i // 2
        if i % 2 == 0:
            x2 = _even_layer(x2, bsz, seq, ev_norm[j], ev_w_in[j], ssm_lam_re[j], ssm_lam_im[j],
                             ssm_log_dt[j], ssm_b_re[j], ssm_b_im[j], ssm_c_re[j], ssm_c_im[j],
                             ssm_d[j], ssm_w_glu[j], ssm_b_glu[j], sg_ln_g[j], sg_ln_b[j],
                             sg_w_sp[j], sg_b_sp[j], ev_w_out[j])
        else:
            lambda_init = 0.8 - 0.6 * math.exp(-0.3 * i)
            x2 = _odd_layer(x2, bsz, seq, od_norm[j], od_w_in[j], da_lq1[j], da_lk1[j], da_lq2[j],
                            da_lk2[j], da_subln[j], od_w_out[j], lambda_init, final_norm,
                            final_norm=(i == depth - 1))
    return x2.reshape(bsz, seq, d)
```

```python
import functools
import math

import jax
import jax.numpy as jnp
from jax import lax
from jax.experimental import pallas as pl
from jax.experimental.pallas import tpu as pltpu

F32 = jnp.float32
BF16 = jnp.bfloat16

RMS_EPS = 1e-6
LN_EPS = 1e-5

SSM_GROUP = 16
SSM_STATE = 64
SSM_CHUNK = 16
S5_TILE_GROUPS = 16
SG_CHUNK = 128
SG_HEADS = 8
DA_HEADS = 8
DA_HEAD_DIM = 128
DA_V_DIM = 256
ROT_DIM = 32
ROPE_THETA = 500000.0
NEG_BIG = -1e30
ATTN_TILE = 512
IN_PROJ_COL_CHUNK = 512

VMEM_LIMIT_BYTES = 56 * 1024 * 1024


def _cparams(*sem):
    return pltpu.CompilerParams(dimension_semantics=sem, vmem_limit_bytes=VMEM_LIMIT_BYTES)


def _gelu_tanh(x):
    return 0.5 * x * (1.0 + jnp.tanh(math.sqrt(2.0 / math.pi) * (x + 0.044715 * (x * x * x))))


def _sigmoid(x):
    return 1.0 / (1.0 + jnp.exp(-x))


def _silu(x):
    return x * _sigmoid(x)


def _in_proj_body(x_ref, g_ref, w_ref, *rest, epilogues, n_aux, chunk_major_first, tn, resident):
    aux = rest[:n_aux]
    outs = rest[n_aux:n_aux + len(epilogues)]
    hn_ref = rest[n_aux + len(epilogues)]
    j = pl.program_id(1)

    def normalise():
        x = x_ref[...]
        ms = jnp.mean(x * x, axis=-1, keepdims=True)
        hn_ref[...] = ((x * lax.rsqrt(ms + RMS_EPS)) * g_ref[...]).astype(BF16)

    def segment(k, ep, cw, w0):
        chunk_major = k == 0 and chunk_major_first
        for c0 in range(0, tn, cw):
            acc = jnp.dot(hn_ref[...], w_ref[:, w0 + c0:w0 + c0 + cw], preferred_element_type=F32)
            val = ep(acc, aux)
            if chunk_major:
                tmp_ref = rest[-1]
                for c in range(cw // 128):
                    tmp_ref[c0 // 128 + c] = val[:, c * 128:(c + 1) * 128]
            else:
                outs[k][:, c0:c0 + cw] = val.astype(outs[k].dtype)
        if chunk_major:
            n_blk, n_rows = tmp_ref.shape[0], tmp_ref.shape[1] // SSM_CHUNK
            for i in range(SSM_CHUNK):
                rows = [tmp_ref[c, pl.ds(i, n_rows, stride=SSM_CHUNK), :] for c in range(n_blk)]
                outs[0][i] = jnp.concatenate(rows, axis=1).astype(outs[0].dtype)

    if resident:
        normalise()
        for k, (ep, col_chunk) in enumerate(epilogues):
            segment(k, ep, col_chunk or tn, k * tn)
    else:
        pl.when(j == 0)(normalise)
        for k, (ep, col_chunk) in enumerate(epilogues):
            pl.when(j == k)(functools.partial(segment, k, ep, col_chunk or tn, 0))


def _in_proj(x2, norm_g, w_bf16, epilogues, out_dtypes, aux=(), aux_specs=(), *, tm, tn, name,
             chunk_major_first=False, resident=False):
    n_tok, d = x2.shape
    n_seg = len(epilogues)
    assert w_bf16.shape == (d, n_seg * tn) and n_tok % tm == 0
    if resident:
        w_spec = pl.BlockSpec((d, n_seg * tn), lambda i, j: (0, 0), pipeline_mode=pl.Buffered(1))
    else:
        w_spec = pl.BlockSpec((d, tn), lambda i, j: (0, j))
    out_shape = [jax.ShapeDtypeStruct((n_tok, tn), dt) for dt in out_dtypes]
    out_specs = [pl.BlockSpec((tm, tn), lambda i, j: (i, 0)) for _ in range(n_seg)]
    scratch = [pltpu.VMEM((tm, d), BF16)]
    if chunk_major_first:
        t = SSM_CHUNK
        out_shape[0] = jax.ShapeDtypeStruct((t, n_tok // t, tn), out_dtypes[0])
        out_specs[0] = pl.BlockSpec((t, tm // t, tn), lambda i, j: (0, i, 0))
        scratch.append(pltpu.VMEM((tn // 128, tm, 128), F32))
    return pl.pallas_call(
        functools.partial(_in_proj_body, epilogues=epilogues, n_aux=len(aux),
                          chunk_major_first=chunk_major_first, tn=tn, resident=resident),
        grid=(n_tok // tm, 1 if resident else n_seg),
        in_specs=[pl.BlockSpec((tm, d), lambda i, j: (i, 0)),
                  pl.BlockSpec((1, d), lambda i, j: (0, 0)),
                  w_spec] + list(aux_specs),
        out_specs=out_specs,
        out_shape=out_shape,
        scratch_shapes=scratch,
        compiler_params=_cparams("parallel", "arbitrary"),
        name=name,
    )(x2, norm_g.reshape(1, d), w_bf16, *aux)


def _ep_identity(acc, aux):
    return acc


def _ep_silu(acc, aux):
    return _silu(acc)


def _ep_gelu(acc, aux):
    return _gelu_tanh(acc)


def _ep_gelu_layernorm(acc, aux):
    ln_g, ln_b = aux[0][...], aux[1][...]
    z = _gelu_tanh(acc)
    mu = jnp.mean(z, axis=-1, keepdims=True)
    zc = z - mu
    y = zc * lax.rsqrt(jnp.mean(zc * zc, axis=-1, keepdims=True) + LN_EPS)
    return y * ln_g + ln_b


def _rope(acc, tab):
    c, s_lo, s_hi = tab[:, 0:128], tab[:, 128:256], tab[:, 256:384]
    pieces = []
    for h in range(acc.shape[1] // DA_HEAD_DIM):
        blk = acc[:, h * DA_HEAD_DIM:(h + 1) * DA_HEAD_DIM]
        rot = (blk * c + pltpu.roll(blk, ROT_DIM // 2, 1) * s_lo
               + pltpu.roll(blk, DA_HEAD_DIM - ROT_DIM // 2, 1) * s_hi)
        pieces.append(rot)
    return jnp.concatenate(pieces, axis=1)


def _ep_rope_scaled(acc, aux):
    return _rope(acc, aux[0][...]) * (DA_HEAD_DIM ** -0.5 * math.log2(math.e))


def _ep_rope(acc, aux):
    return _rope(acc, aux[0][...])


def _rope_table(seq):
    pos = jnp.arange(seq, dtype=F32)
    inv_freq = ROPE_THETA ** (-jnp.arange(0, ROT_DIM, 2, dtype=F32) / ROT_DIM)
    ang = pos[:, None] * inv_freq[None, :]
    cos, sin = jnp.cos(ang), jnp.sin(ang)
    half = ROT_DIM // 2
    pad = DA_HEAD_DIM - ROT_DIM
    c = jnp.concatenate([cos, cos, jnp.ones((seq, pad), F32)], axis=1)
    s_lo = jnp.concatenate([jnp.zeros((seq, half), F32), sin, jnp.zeros((seq, pad), F32)], axis=1)
    s_hi = jnp.concatenate([-sin, jnp.zeros((seq, half + pad), F32)], axis=1)
    return jnp.concatenate([c, s_lo, s_hi], axis=1)


def _s5_prep_body(lr_ref, li_ref, ldt_ref, bre_ref, bim_ref, bbre_ref, bbim_ref, a_ref, pw_ref, *, n_pow):
    lr, li = lr_ref[...], li_ref[...]
    dt = jnp.exp(ldt_ref[...])
    mag = jnp.exp(lr * dt)
    a_re = mag * jnp.cos(li * dt)
    a_im = mag * jnp.sin(li * dt)
    den = lr * lr + li * li
    nr = a_re - 1.0
    f_re = (nr * lr + a_im * li) / den
    f_im = (a_im * lr - nr * li) / den
    b_re, b_im = bre_ref[...], bim_ref[...]
    bbre_ref[...] = f_re * b_re - f_im * b_im
    bbim_ref[...] = f_re * b_im + f_im * b_re
    a_ref[0:1, :] = a_re
    a_ref[1:2, :] = a_im
    mag_t = jnp.exp(lr * dt * SSM_CHUNK)
    pr = mag_t * jnp.cos(li * dt * SSM_CHUNK)
    pi_ = mag_t * jnp.sin(li * dt * SSM_CHUNK)
    for k in range(n_pow):
        pw_ref[k:k + 1, :] = pr
        pw_ref[n_pow + k:n_pow + k + 1, :] = pi_
        pr, pi_ = pr * pr - pi_ * pi_, 2.0 * pr * pi_


def _s5_prep(lam_re, lam_im, log_dt, b_re, b_im, n_pow):
    g, p = lam_re.shape
    h = b_re.shape[2]
    row = lambda a: a.reshape(1, g * p)
    chan = lambda b: b.transpose(2, 0, 1).reshape(h, g * p)
    ins = [row(lam_re), row(lam_im), row(jnp.broadcast_to(log_dt[:, None], (g, p))), chan(b_re), chan(b_im)]
    out_shape = [jax.ShapeDtypeStruct((h, g * p), F32), jax.ShapeDtypeStruct((h, g * p), F32),
                 jax.ShapeDtypeStruct((2, g * p), F32), jax.ShapeDtypeStruct((2 * n_pow, g * p), F32)]
    return pl.pallas_call(
        functools.partial(_s5_prep_body, n_pow=n_pow),
        out_shape=out_shape,
        compiler_params=pltpu.CompilerParams(vmem_limit_bytes=VMEM_LIMIT_BYTES),
        name="s5_prep",
    )(*ins)


def _block_diag(w4):
    eye = jnp.eye(w4.shape[1], dtype=w4.dtype)
    return w4[:, :, :, None, :] * eye[None, :, None, :, None]


def _s5_operators(bb_re, bb_im, a, pw, c_re, c_im, d_skip, n_pow):
    g, h, p = c_re.shape
    gt = S5_TILE_GROUPS
    nq = g // gt
    sw = gt * p

    def b_blk(bb):
        w4 = bb.reshape(h, nq, gt, p).transpose(1, 2, 0, 3)
        return _block_diag(w4).reshape(nq, gt * h, sw)

    def c_blk(c):
        w4 = c.reshape(nq, gt, h, p)
        return _block_diag(w4).transpose(0, 3, 4, 1, 2).reshape(nq, sw, gt * h)

    bblk = jnp.concatenate([b_blk(bb_re), b_blk(bb_im)], axis=2).astype(BF16)
    cblk = jnp.concatenate([c_blk(c_re), -c_blk(c_im)], axis=1).astype(BF16)
    a_rows = a.reshape(2, nq, sw).transpose(1, 0, 2)
    pw_rows = pw.reshape(2 * n_pow, nq, sw).transpose(1, 0, 2)
    d_rows = d_skip.reshape(nq, 1, gt * h)
    return bblk, cblk, a_rows, pw_rows, d_rows


def _s5_body(x_ref, bblk_ref, cblk_ref, a_ref, pw_ref, d_ref, y_ref, h_ref, *, n_pow):
    t = x_ref.shape[0]
    n_rows = x_ref.shape[1]
    half = h_ref.shape[1] // 2
    a_re, a_im = a_ref[0, 0:1, :], a_ref[0, 1:2, :]

    def advance(i):
        bu = jnp.dot(x_ref[i], bblk_ref[0], preferred_element_type=F32)
        h_re, h_im = h_ref[:, 0:half], h_ref[:, half:]
        n_re = a_re * h_re - a_im * h_im + bu[:, 0:half]
        n_im = a_re * h_im + a_im * h_re + bu[:, half:]
        h_ref[:, 0:half] = n_re
        h_ref[:, half:] = n_im
        return n_re, n_im

    h_ref[...] = jnp.zeros(h_ref.shape, F32)

    def local_step(i, c):
        advance(i)
        return c

    lax.fori_loop(0, t, local_step, 0, unroll=2)

    s_re, s_im = h_ref[:, 0:half], h_ref[:, half:]
    cidx = lax.broadcasted_iota(jnp.int32, (n_rows, half), 0)
    for k in range(n_pow):
        d = 1 << k
        p_re = jnp.where(cidx >= d, pltpu.roll(s_re, d, 0), 0.0)
        p_im = jnp.where(cidx >= d, pltpu.roll(s_im, d, 0), 0.0)
        w_re, w_im = pw_ref[0, k:k + 1, :], pw_ref[0, n_pow + k:n_pow + k + 1, :]
        s_re, s_im = s_re + w_re * p_re - w_im * p_im, s_im + w_re * p_im + w_im * p_re
    h_ref[:, 0:half] = jnp.where(cidx >= 1, pltpu.roll(s_re, 1, 0), 0.0)
    h_ref[:, half:] = jnp.where(cidx >= 1, pltpu.roll(s_im, 1, 0), 0.0)

    def emit_step(i, c):
        n_re, n_im = advance(i)
        hb = jnp.concatenate([n_re, n_im], axis=1).astype(BF16)
        y = jnp.dot(hb, cblk_ref[0], preferred_element_type=F32)
        y_ref[i] = y + d_ref[0] * x_ref[i].astype(F32)
        return c

    lax.fori_loop(0, t, emit_step, 0, unroll=2)


def _s5(xa_cm, bblk, cblk, a_rows, pw_rows, d_rows, bsz, n_pow):
    t, n_rows, w = xa_cm.shape
    nq, tile, s2 = bblk.shape
    rows_b = n_rows // bsz
    per_q = lambda a: pl.BlockSpec((1,) + a.shape[1:], lambda q, b: (q, 0, 0))
    act = pl.BlockSpec((t, rows_b, tile), lambda q, b: (0, b, q))
    return pl.pallas_call(
        functools.partial(_s5_body, n_pow=n_pow),
        grid=(nq, bsz),
        in_specs=[act, per_q(bblk), per_q(cblk), per_q(a_rows), per_q(pw_rows), per_q(d_rows)],
        out_specs=act,
        out_shape=jax.ShapeDtypeStruct((t, n_rows, w), F32),
        scratch_shapes=[pltpu.VMEM((rows_b, s2), F32)],
        compiler_params=_cparams("parallel", "parallel"),
        name="s5_scan",
    )(xa_cm, bblk, cblk, a_rows, pw_rows, d_rows)


def _even_out_body(y_ref, ga_ref, u_ref, v_ref, gb_ref, x_ref, wglu_ref, bglu_ref, wsp_ref, bsp_ref,
                   wout_ref, o_ref, mix_ref, ytok_ref, *, tm):
    n_blk = ytok_ref.shape[0]
    w = n_blk * 128
    for i in range(SSM_CHUNK):
        for c in range(n_blk):
            ytok_ref[c, pl.ds(i, tm // SSM_CHUNK, stride=SSM_CHUNK), :] = y_ref[i, :, c * 128:(c + 1) * 128]
    ya = _gelu_tanh(jnp.concatenate([ytok_ref[c] for c in range(n_blk)], axis=1))
    z = jnp.dot(ya.astype(BF16), wglu_ref[...], preferred_element_type=F32) + bglu_ref[...]
    ya = ya * _sigmoid(z)
    ya = ya * ga_ref[...].astype(F32)
    mix_ref[:, 0:w] = ya.astype(BF16)

    tri = (lax.broadcasted_iota(jnp.int32, (SG_CHUNK, SG_CHUNK), 1)
           <= lax.broadcasted_iota(jnp.int32, (SG_CHUNK, SG_CHUNK), 0))
    hd = w // SG_HEADS
    for g in range(SG_HEADS):
        w_c = jnp.where(tri, wsp_ref[g], 0.0).astype(BF16)
        bias = bsp_ref[:, g:g + 1]
        for n in range(tm // SG_CHUNK):
            rs = slice(n * SG_CHUNK, (n + 1) * SG_CHUNK)
            cs = slice(g * hd, (g + 1) * hd)
            s = jnp.dot(w_c, v_ref[rs, cs], preferred_element_type=F32) + bias
            yb = u_ref[rs, cs].astype(F32) * s * gb_ref[rs, cs].astype(F32)
            mix_ref[rs, w + g * hd:w + (g + 1) * hd] = yb.astype(BF16)

    o_ref[...] = x_ref[...] + jnp.dot(mix_ref[...], wout_ref[...], preferred_element_type=F32)


def _even_out(y, ga, u, v, gb, x2, w_glu, b_glu, w_sp, b_sp_t, w_out, *, tm):
    n_tok, d = x2.shape
    t, _, w = y.shape
    tok = lambda width: pl.BlockSpec((tm, width), lambda i: (i, 0))
    full = lambda a: pl.BlockSpec(a.shape, lambda i: (0,) * a.ndim)
    return pl.pallas_call(
        functools.partial(_even_out_body, tm=tm),
        grid=(n_tok // tm,),
        in_specs=[pl.BlockSpec((t, tm // t, w), lambda i: (0, i, 0)), tok(w), tok(w), tok(w), tok(w), tok(d),
                  full(w_glu), full(b_glu), full(w_sp), full(b_sp_t), full(w_out)],
        out_specs=tok(d),
        out_shape=jax.ShapeDtypeStruct((n_tok, d), F32),
        scratch_shapes=[pltpu.VMEM((tm, 2 * w), BF16), pltpu.VMEM((w // 128, tm, 128), F32)],
        compiler_params=_cparams("parallel"),
        name="even_out",
    )(y, ga, u, v, gb, x2, w_glu, b_glu, w_sp, b_sp_t, w_out)


def _attn_body(lq1_ref, lk1_ref, lq2_ref, lk2_ref, sg_ref, q_ref, k_ref, v_ref, o_ref, *scratch,
               tq, lambda_init):
    lam = (jnp.exp(jnp.sum(lq1_ref[...] * lk1_ref[...])) - jnp.exp(jnp.sum(lq2_ref[...] * lk2_ref[...]))
           + lambda_init)

    def q_tile(qi, c):
        _attn_tile(qi, lam, sg_ref, q_ref, k_ref, v_ref, o_ref, *scratch, tq=tq, lambda_init=lambda_init)
        return c

    lax.fori_loop(0, q_ref.shape[1] // tq, q_tile, 0)


def _attn_tile(qi, lam, sg_ref, q_ref, k_ref, v_ref, o_ref, s_ref, mx_ref, mrep_ref, ls_ref, acc_ref, *,
               tq, lambda_init):
    hd = DA_HEAD_DIM
    n_lane = tq // 128
    q0 = pl.multiple_of(qi * tq, tq)
    q = q_ref[0, pl.ds(q0, tq), :]
    qs = (q[:, 0:hd], q[:, hd:2 * hd])
    mx_ref[...] = jnp.full(mx_ref.shape, NEG_BIG, F32)
    ls_ref[...] = jnp.zeros(ls_ref.shape, F32)
    acc_ref[...] = jnp.zeros(acc_ref.shape, F32)

    def lane_fold(x, op):
        out = x[:, 0:128]
        for c in range(1, n_lane):
            out = op(out, x[:, c * 128:(c + 1) * 128])
        return out

    def scores(j, masked):
        k0 = pl.multiple_of(j * tq, tq)
        kj = k_ref[0, pl.ds(k0, tq), :]
        for m in range(2):
            s = lax.dot_general(qs[m], kj[:, m * hd:(m + 1) * hd], (((1,), (1,)), ((), ())),
                                preferred_element_type=F32)
            if masked:
                keep = (lax.broadcasted_iota(jnp.int32, (tq, tq), 1)
                        <= lax.broadcasted_iota(jnp.int32, (tq, tq), 0))
                s = jnp.where(keep, s, NEG_BIG)
            s_ref[m, j] = s
            mx_ref[m] = jnp.maximum(mx_ref[m], lane_fold(s, jnp.maximum))

    def run_blocks(n_blk, fn):
        def quad(jg, c):
            for u in range(4):
                fn(4 * jg + u)
            return c

        lax.fori_loop(0, n_blk // 4, quad, 0)

        @pl.when(n_blk % 4 >= 2)
        def _():
            base = (n_blk // 4) * 4
            fn(base)
            fn(base + 1)

        @pl.when(n_blk % 2 == 1)
        def _():
            fn(n_blk - 1)

    run_blocks(qi, functools.partial(scores, masked=False))
    scores(qi, True)

    for m in range(2):
        mrep_ref[m] = jnp.broadcast_to(jnp.max(mx_ref[m], axis=-1, keepdims=True), (tq, 128))

    def accumulate(j):
        k0 = pl.multiple_of(j * tq, tq)
        vj = v_ref[0, pl.ds(k0, tq), :]
        for m in range(2):
            s = s_ref[m, j]
            mrep = mrep_ref[m]
            p = jnp.concatenate([jnp.exp2(s[:, c_ * 128:(c_ + 1) * 128] - mrep) for c_ in range(n_lane)],
                                axis=1)
            ls_ref[m] += lane_fold(p, jnp.add)
            acc_ref[m] += jnp.dot(p.astype(BF16), vj, preferred_element_type=F32)

    run_blocks(qi + 1, accumulate)

    o1 = acc_ref[0] / jnp.sum(ls_ref[0], axis=-1, keepdims=True)
    o2 = acc_ref[1] / jnp.sum(ls_ref[1], axis=-1, keepdims=True)
    o = o1 - lam * o2
    ms = jnp.mean(o * o, axis=-1, keepdims=True)
    o = (o * lax.rsqrt(ms + RMS_EPS)) * sg_ref[...]
    o_ref[0, pl.ds(q0, tq), :] = (o * (1.0 - lambda_init)).astype(o_ref.dtype)


def _diff_attn(q, k, v, lq1, lk1, lq2, lk2, subln_g, lambda_init, *, tq):
    b, seq, width = q.shape
    vec = lambda a: a.reshape(1, -1)
    small = lambda n: pl.BlockSpec((1, n), lambda bi, hi: (0, 0))
    head = pl.BlockSpec((1, seq, DA_V_DIM), lambda bi, hi: (bi, 0, hi))
    scratch = [pltpu.VMEM((2, seq // tq, tq, tq), F32),
               pltpu.VMEM((2, tq, 128), F32),
               pltpu.VMEM((2, tq, 128), F32),
               pltpu.VMEM((2, tq, 128), F32),
               pltpu.VMEM((2, tq, DA_V_DIM), F32)]
    return pl.pallas_call(
        functools.partial(_attn_body, tq=tq, lambda_init=lambda_init),
        grid=(b, DA_HEADS),
        in_specs=[small(DA_HEAD_DIM)] * 4 + [small(DA_V_DIM), head, head, head],
        out_specs=head,
        out_shape=jax.ShapeDtypeStruct((b, seq, width), BF16),
        scratch_shapes=scratch,
        compiler_params=_cparams("parallel", "parallel"),
        name="diff_attn",
    )(vec(lq1), vec(lk1), vec(lq2), vec(lk2), vec(subln_g), q, k, v)


def _odd_out_body(o_ref, g_ref, x_ref, w_ref, fn_ref, out_ref, *, final_norm):
    gated = (o_ref[...].astype(F32) * g_ref[...].astype(F32)).astype(BF16)
    y = x_ref[...] + jnp.dot(gated, w_ref[...], preferred_element_type=F32)
    if final_norm:
        ms = jnp.mean(y * y, axis=-1, keepdims=True)
        y = (y * lax.rsqrt(ms + RMS_EPS)) * fn_ref[...]
    out_ref[...] = y


def _odd_out(o, g, x2, w_out, final_g, *, tm, final_norm):
    n_tok, d = x2.shape
    tok = lambda dt: pl.BlockSpec((tm, d), lambda i: (i, 0))
    return pl.pallas_call(
        functools.partial(_odd_out_body, final_norm=final_norm),
        grid=(n_tok // tm,),
        in_specs=[tok(BF16), tok(BF16), tok(F32),
                  pl.BlockSpec(w_out.shape, lambda i: (0, 0)),
                  pl.BlockSpec((1, d), lambda i: (0, 0))],
        out_specs=tok(F32),
        out_shape=jax.ShapeDtypeStruct((n_tok, d), F32),
        compiler_params=_cparams("parallel"),
        name="odd_out",
    )(o, g, x2, w_out, final_g.reshape(1, d))


def _even_layer(x2, bsz, seq, norm_g, w_in, lam_re, lam_im, log_dt, b_re, b_im, c_re, c_im, d_skip,
                w_glu, b_glu, ln_g, ln_b, w_sp, b_sp, w_out):
    n_tok, d = x2.shape
    w = d // 2
    n_chunks = seq // SSM_CHUNK
    n_pow = max(1, (n_chunks - 1).bit_length())

    full = lambda a: pl.BlockSpec(a.shape, lambda i, j: (0, 0))
    ln_g2, ln_b2 = ln_g.reshape(1, w), ln_b.reshape(1, w)
    xa_cm, ga, u, v, gb = _in_proj(
        x2, norm_g, w_in.astype(BF16),
        [(_ep_identity, IN_PROJ_COL_CHUNK), (_ep_silu, IN_PROJ_COL_CHUNK), (_ep_gelu, IN_PROJ_COL_CHUNK),
         (_ep_gelu_layernorm, None), (_ep_silu, IN_PROJ_COL_CHUNK)],
        [BF16] * 5, aux=(ln_g2, ln_b2), aux_specs=(full(ln_g2), full(ln_b2)),
        tm=512, tn=w, name="even_in_proj", chunk_major_first=True, resident=True)

    bb_re, bb_im, a, pw = _s5_prep(lam_re, lam_im, log_dt, b_re, b_im, n_pow)
    ops = _s5_operators(bb_re, bb_im, a, pw, c_re, c_im, d_skip, n_pow)
    y = _s5(xa_cm, *ops, bsz, n_pow)

    return _even_out(y, ga, u, v, gb, x2, w_glu.astype(BF16), b_glu.reshape(1, w),
                     w_sp, b_sp.T, w_out.astype(BF16), tm=512)


def _odd_layer(x2, bsz, seq, norm_g, w_in, lq1, lk1, lq2, lk2, subln_g, w_out, lambda_init,
               final_g, final_norm):
    n_tok, d = x2.shape
    tm = 512
    tab = _rope_table(seq)
    n_pos_blocks = seq // tm
    tab_spec = pl.BlockSpec((tm, tab.shape[1]), lambda i, j: (i % n_pos_blocks, 0))
    q, k, v, g = _in_proj(
        x2, norm_g, w_in.astype(BF16),
        [(_ep_rope_scaled, IN_PROJ_COL_CHUNK), (_ep_rope, IN_PROJ_COL_CHUNK),
         (_ep_identity, IN_PROJ_COL_CHUNK), (_ep_silu, IN_PROJ_COL_CHUNK)],
        [BF16] * 4, aux=(tab,), aux_specs=(tab_spec,), tm=tm, tn=d, name="odd_in_proj")
    shp = (bsz, seq, d)
    o = _diff_attn(q.reshape(shp), k.reshape(shp), v.reshape(shp), lq1, lk1, lq2, lk2, subln_g,
                   lambda_init, tq=ATTN_TILE)
    return _odd_out(o.reshape(n_tok, d), g, x2, w_out.astype(BF16), final_g, tm=512,
                    final_norm=final_norm)


def kernel(x, ev_norm, ev_w_in, ssm_lam_re, ssm_lam_im, ssm_log_dt, ssm_b_re, ssm_b_im, ssm_c_re, ssm_c_im, ssm_d, ssm_w_glu, ssm_b_glu, sg_ln_g, sg_ln_b, sg_w_sp, sg_b_sp, ev_w_out, od_norm, od_w_in, da_lq1, da_lk1, da_lq2, da_lk2, da_subln, od_w_out, final_norm):
    bsz, seq, d = x.shape
    depth = ev_norm.shape[0] + od_norm.shape[0]
    assert depth % 2 == 0, "the final rmsnorm is fused into the last (odd) layer"
    x2 = x.reshape(bsz * seq, d)
    for i in range(depth):
        j = i // 2
        if i % 2 == 0:
            x2 = _even_layer(x2, bsz, seq, ev_norm[j], ev_w_in[j], ssm_lam_re[j], ssm_lam_im[j],
                             ssm_log_dt[j], ssm_b_re[j], ssm_b_im[j], ssm_c_re[j], ssm_c_im[j],
                             ssm_d[j], ssm_w_glu[j], ssm_b_glu[j], sg_ln_g[j], sg_ln_b[j],
                             sg_w_sp[j], sg_b_sp[j], ev_w_out[j])
        else:
            lambda_init = 0.8 - 0.6 * math.exp(-0.3 * i)
            x2 = _odd_layer(x2, bsz, seq, od_norm[j], od_w_in[j], da_lq1[j], da_lk1[j], da_lq2[j],
                            da_lk2[j], da_subln[j], od_w_out[j], lambda_init, final_norm,
                            final_norm=(i == depth - 1))
    return x2.reshape(bsz, seq, d)
```

```python
import functools
import math

import jax
import jax.numpy as jnp
from jax import lax
from jax.experimental import pallas as pl
from jax.experimental.pallas import tpu as pltpu

F32 = jnp.float32
BF16 = jnp.bfloat16

RMS_EPS = 1e-6
LN_EPS = 1e-5

SSM_GROUP = 16
SSM_STATE = 64
SSM_CHUNK = 16
S5_TILE_GROUPS = 16
SG_CHUNK = 128
SG_HEADS = 8
DA_HEADS = 8
DA_HEAD_DIM = 128
DA_V_DIM = 256
ROT_DIM = 32
ROPE_THETA = 500000.0
NEG_BIG = -1e30
ATTN_TILE = 512
IN_PROJ_COL_CHUNK = 512

VMEM_LIMIT_BYTES = 56 * 1024 * 1024


def _cparams(*sem):
    return pltpu.CompilerParams(dimension_semantics=sem, vmem_limit_bytes=VMEM_LIMIT_BYTES)


def _gelu_tanh(x):
    return 0.5 * x * (1.0 + jnp.tanh(math.sqrt(2.0 / math.pi) * (x + 0.044715 * (x * x * x))))


def _sigmoid(x):
    return 1.0 / (1.0 + jnp.exp(-x))


def _silu(x):
    return x * _sigmoid(x)


def _in_proj_body(x_ref, g_ref, w_ref, *rest, epilogues, n_aux, chunk_major_first, tn, resident):
    aux = rest[:n_aux]
    outs = rest[n_aux:n_aux + len(epilogues)]
    hn_ref = rest[n_aux + len(epilogues)]
    j = pl.program_id(1)

    def normalise():
        x = x_ref[...]
        ms = jnp.mean(x * x, axis=-1, keepdims=True)
        hn_ref[...] = ((x * lax.rsqrt(ms + RMS_EPS)) * g_ref[...]).astype(BF16)

    def segment(k, ep, cw, w0):
        chunk_major = k == 0 and chunk_major_first
        for c0 in range(0, tn, cw):
            acc = jnp.dot(hn_ref[...], w_ref[:, w0 + c0:w0 + c0 + cw], preferred_element_type=F32)
            val = ep(acc, aux)
            if chunk_major:
                tmp_ref = rest[-1]
                for c in range(cw // 128):
                    tmp_ref[c0 // 128 + c] = val[:, c * 128:(c + 1) * 128]
            else:
                outs[k][:, c0:c0 + cw] = val.astype(outs[k].dtype)
        if chunk_major:
            n_blk, n_rows = tmp_ref.shape[0], tmp_ref.shape[1] // SSM_CHUNK
            for i in range(SSM_CHUNK):
                rows = [tmp_ref[c, pl.ds(i, n_rows, stride=SSM_CHUNK), :] for c in range(n_blk)]
                outs[0][i] = jnp.concatenate(rows, axis=1).astype(outs[0].dtype)

    if resident:
        normalise()
        for k, (ep, col_chunk) in enumerate(epilogues):
            segment(k, ep, col_chunk or tn, k * tn)
    else:
        pl.when(j == 0)(normalise)
        for k, (ep, col_chunk) in enumerate(epilogues):
            pl.when(j == k)(functools.partial(segment, k, ep, col_chunk or tn, 0))


def _in_proj(x2, norm_g, w_bf16, epilogues, out_dtypes, aux=(), aux_specs=(), *, tm, tn, name,
             chunk_major_first=False, resident=False):
    n_tok, d = x2.shape
    n_seg = len(epilogues)
    assert w_bf16.shape == (d, n_seg * tn) and n_tok % tm == 0
    if resident:
        w_spec = pl.BlockSpec((d, n_seg * tn), lambda i, j: (0, 0), pipeline_mode=pl.Buffered(1))
    else:
        w_spec = pl.BlockSpec((d, tn), lambda i, j: (0, j))
    out_shape = [jax.ShapeDtypeStruct((n_tok, tn), dt) for dt in out_dtypes]
    out_specs = [pl.BlockSpec((tm, tn), lambda i, j: (i, 0)) for _ in range(n_seg)]
    scratch = [pltpu.VMEM((tm, d), BF16)]
    if chunk_major_first:
        t = SSM_CHUNK
        out_shape[0] = jax.ShapeDtypeStruct((t, n_tok // t, tn), out_dtypes[0])
        out_specs[0] = pl.BlockSpec((t, tm // t, tn), lambda i, j: (0, i, 0))
        scratch.append(pltpu.VMEM((tn // 128, tm, 128), F32))
    return pl.pallas_call(
        functools.partial(_in_proj_body, epilogues=epilogues, n_aux=len(aux),
                          chunk_major_first=chunk_major_first, tn=tn, resident=resident),
        grid=(n_tok // tm, 1 if resident else n_seg),
        in_specs=[pl.BlockSpec((tm, d), lambda i, j: (i, 0)),
                  pl.BlockSpec((1, d), lambda i, j: (0, 0)),
                  w_spec] + list(aux_specs),
        out_specs=out_specs,
        out_shape=out_shape,
        scratch_shapes=scratch,
        compiler_params=_cparams("parallel", "arbitrary"),
        name=name,
    )(x2, norm_g.reshape(1, d), w_bf16, *aux)


def _ep_identity(acc, aux):
    return acc


def _ep_silu(acc, aux):
    return _silu(acc)


def _ep_gelu(acc, aux):
    return _gelu_tanh(acc)


def _ep_gelu_layernorm(acc, aux):
    ln_g, ln_b = aux[0][...], aux[1][...]
    z = _gelu_tanh(acc)
    mu = jnp.mean(z, axis=-1, keepdims=True)
    zc = z - mu
    y = zc * lax.rsqrt(jnp.mean(zc * zc, axis=-1, keepdims=True) + LN_EPS)
    return y * ln_g + ln_b


def _rope(acc, tab):
    c, s_lo, s_hi = tab[:, 0:128], tab[:, 128:256], tab[:, 256:384]
    pieces = []
    for h in range(acc.shape[1] // DA_HEAD_DIM):
        blk = acc[:, h * DA_HEAD_DIM:(h + 1) * DA_HEAD_DIM]
        rot = (blk * c + pltpu.roll(blk, ROT_DIM // 2, 1) * s_lo
               + pltpu.roll(blk, DA_HEAD_DIM - ROT_DIM // 2, 1) * s_hi)
        pieces.append(rot)
    return jnp.concatenate(pieces, axis=1)


def _ep_rope_scaled(acc, aux):
    return _rope(acc, aux[0][...]) * (DA_HEAD_DIM ** -0.5 * math.log2(math.e))


def _ep_rope(acc, aux):
    return _rope(acc, aux[0][...])


def _rope_table(seq):
    pos = jnp.arange(seq, dtype=F32)
    inv_freq = ROPE_THETA ** (-jnp.arange(0, ROT_DIM, 2, dtype=F32) / ROT_DIM)
    ang = pos[:, None] * inv_freq[None, :]
    cos, sin = jnp.cos(ang), jnp.sin(ang)
    half = ROT_DIM // 2
    pad = DA_HEAD_DIM - ROT_DIM
    c = jnp.concatenate([cos, cos, jnp.ones((seq, pad), F32)], axis=1)
    s_lo = jnp.concatenate([jnp.zeros((seq, half), F32), sin, jnp.zeros((seq, pad), F32)], axis=1)
    s_hi = jnp.concatenate([-sin, jnp.zeros((seq, half + pad), F32)], axis=1)
    return jnp.concatenate([c, s_lo, s_hi], axis=1)


def _s5_prep_body(lr_ref, li_ref, ldt_ref, bre_ref, bim_ref, bbre_ref, bbim_ref, a_ref, pw_ref, *, n_pow):
    lr, li = lr_ref[...], li_ref[...]
    dt = jnp.exp(ldt_ref[...])
    mag = jnp.exp(lr * dt)
    a_re = mag * jnp.cos(li * dt)
    a_im = mag * jnp.sin(li * dt)
    den = lr * lr + li * li
    nr = a_re - 1.0
    f_re = (nr * lr + a_im * li) / den
    f_im = (a_im * lr - nr * li) / den
    b_re, b_im = bre_ref[...], bim_ref[...]
    bbre_ref[...] = f_re * b_re - f_im * b_im
    bbim_ref[...] = f_re * b_im + f_im * b_re
    a_ref[0:1, :] = a_re
    a_ref[1:2, :] = a_im
    mag_t = jnp.exp(lr * dt * SSM_CHUNK)
    pr = mag_t * jnp.cos(li * dt * SSM_CHUNK)
    pi_ = mag_t * jnp.sin(li * dt * SSM_CHUNK)
    for k in range(n_pow):
        pw_ref[k:k + 1, :] = pr
        pw_ref[n_pow + k:n_pow + k + 1, :] = pi_
        pr, pi_ = pr * pr - pi_ * pi_, 2.0 * pr * pi_


def _s5_prep(lam_re, lam_im, log_dt, b_re, b_im, n_pow):
    g, p = lam_re.shape
    h = b_re.shape[2]
    row = lambda a: a.reshape(1, g * p)
    chan = lambda b: b.transpose(2, 0, 1).reshape(h, g * p)
    ins = [row(lam_re), row(lam_im), row(jnp.broadcast_to(log_dt[:, None], (g, p))), chan(b_re), chan(b_im)]
    out_shape = [jax.ShapeDtypeStruct((h, g * p), F32), jax.ShapeDtypeStruct((h, g * p), F32),
                 jax.ShapeDtypeStruct((2, g * p), F32), jax.ShapeDtypeStruct((2 * n_pow, g * p), F32)]
    return pl.pallas_call(
        functools.partial(_s5_prep_body, n_pow=n_pow),
        out_shape=out_shape,
        compiler_params=pltpu.CompilerParams(vmem_limit_bytes=VMEM_LIMIT_BYTES),
        name="s5_prep",
    )(*ins)


def _block_diag(w4):
    eye = jnp.eye(w4.shape[1], dtype=w4.dtype)
    return w4[:, :, :, None, :] * eye[None, :, None, :, None]


def _s5_operators(bb_re, bb_im, a, pw, c_re, c_im, d_skip, n_pow):
    g, h, p = c_re.shape
    gt = S5_TILE_GROUPS
    nq = g // gt
    sw = gt * p

    def b_blk(bb):
        w4 = bb.reshape(h, nq, gt, p).transpose(1, 2, 0, 3)
        return _block_diag(w4).reshape(nq, gt * h, sw)

    def c_blk(c):
        w4 = c.reshape(nq, gt, h, p)
        return _block_diag(w4).transpose(0, 3, 4, 1, 2).reshape(nq, sw, gt * h)

    bblk = jnp.concatenate([b_blk(bb_re), b_blk(bb_im)], axis=2).astype(BF16)
    cblk = jnp.concatenate([c_blk(c_re), -c_blk(c_im)], axis=1).astype(BF16)
    a_rows = a.reshape(2, nq, sw).transpose(1, 0, 2)
    pw_rows = pw.reshape(2 * n_pow, nq, sw).transpose(1, 0, 2)
    d_rows = d_skip.reshape(nq, 1, gt * h)
    return bblk, cblk, a_rows, pw_rows, d_rows


def _s5_body(x_ref, bblk_ref, cblk_ref, a_ref, pw_ref, d_ref, y_ref, h_ref, *, n_pow):
    t = x_ref.shape[0]
    n_rows = x_ref.shape[1]
    half = h_ref.shape[1] // 2
    a_re, a_im = a_ref[0, 0:1, :], a_ref[0, 1:2, :]

    def advance(i):
        bu = jnp.dot(x_ref[i], bblk_ref[0], preferred_element_type=F32)
        h_re, h_im = h_ref[:, 0:half], h_ref[:, half:]
        n_re = a_re * h_re - a_im * h_im + bu[:, 0:half]
        n_im = a_re * h_im + a_im * h_re + bu[:, half:]
        h_ref[:, 0:half] = n_re
        h_ref[:, half:] = n_im
        return n_re, n_im

    h_ref[...] = jnp.zeros(h_ref.shape, F32)

    def local_step(i, c):
        advance(i)
        return c

    lax.fori_loop(0, t, local_step, 0, unroll=2)

    s_re, s_im = h_ref[:, 0:half], h_ref[:, half:]
    cidx = lax.broadcasted_iota(jnp.int32, (n_rows, half), 0)
    for k in range(n_pow):
        d = 1 << k
        p_re = jnp.where(cidx >= d, pltpu.roll(s_re, d, 0), 0.0)
        p_im = jnp.where(cidx >= d, pltpu.roll(s_im, d, 0), 0.0)
        w_re, w_im = pw_ref[0, k:k + 1, :], pw_ref[0, n_pow + k:n_pow + k + 1, :]
        s_re, s_im = s_re + w_re * p_re - w_im * p_im, s_im + w_re * p_im + w_im * p_re
    h_ref[:, 0:half] = jnp.where(cidx >= 1, pltpu.roll(s_re, 1, 0), 0.0)
    h_ref[:, half:] = jnp.where(cidx >= 1, pltpu.roll(s_im, 1, 0), 0.0)

    def emit_step(i, c):
        n_re, n_im = advance(i)
        hb = jnp.concatenate([n_re, n_im], axis=1).astype(BF16)
        y = jnp.dot(hb, cblk_ref[0], preferred_element_type=F32)
        y_ref[i] = y + d_ref[0] * x_ref[i].astype(F32)
        return c

    lax.fori_loop(0, t, emit_step, 0, unroll=2)


def _s5(xa_cm, bblk, cblk, a_rows, pw_rows, d_rows, bsz, n_pow):
    t, n_rows, w = xa_cm.shape
    nq, tile, s2 = bblk.shape
    rows_b = n_rows // bsz
    per_q = lambda a: pl.BlockSpec((1,) + a.shape[1:], lambda q, b: (q, 0, 0))
    act = pl.BlockSpec((t, rows_b, tile), lambda q, b: (0, b, q))
    return pl.pallas_call(
        functools.partial(_s5_body, n_pow=n_pow),
        grid=(nq, bsz),
        in_specs=[act, per_q(bblk), per_q(cblk), per_q(a_rows), per_q(pw_rows), per_q(d_rows)],
        out_specs=act,
        out_shape=jax.ShapeDtypeStruct((t, n_rows, w), F32),
        scratch_shapes=[pltpu.VMEM((rows_b, s2), F32)],
        compiler_params=_cparams("parallel", "parallel"),
        name="s5_scan",
    )(xa_cm, bblk, cblk, a_rows, pw_rows, d_rows)


def _even_out_body(y_ref, ga_ref, u_ref, v_ref, gb_ref, x_ref, wglu_ref, bglu_ref, wsp_ref, bsp_ref,
                   wout_ref, o_ref, mix_ref, ytok_ref, *, tm):
    n_blk = ytok_ref.shape[0]
    w = n_blk * 128
    for i in range(SSM_CHUNK):
        for c in range(n_blk):
            ytok_ref[c, pl.ds(i, tm // SSM_CHUNK, stride=SSM_CHUNK), :] = y_ref[i, :, c * 128:(c + 1) * 128]
    ya = _gelu_tanh(jnp.concatenate([ytok_ref[c] for c in range(n_blk)], axis=1))
    z = jnp.dot(ya.astype(BF16), wglu_ref[...], preferred_element_type=F32) + bglu_ref[...]
    ya = ya * _sigmoid(z)
    ya = ya * ga_ref[...].astype(F32)
    mix_ref[:, 0:w] = ya.astype(BF16)

    tri = (lax.broadcasted_iota(jnp.int32, (SG_CHUNK, SG_CHUNK), 1)
           <= lax.broadcasted_iota(jnp.int32, (SG_CHUNK, SG_CHUNK), 0))
    hd = w // SG_HEADS
    for g in range(SG_HEADS):
        w_c = jnp.where(tri, wsp_ref[g], 0.0).astype(BF16)
        bias = bsp_ref[:, g:g + 1]
        for n in range(tm // SG_CHUNK):
            rs = slice(n * SG_CHUNK, (n + 1) * SG_CHUNK)
            cs = slice(g * hd, (g + 1) * hd)
            s = jnp.dot(w_c, v_ref[rs, cs], preferred_element_type=F32) + bias
            yb = u_ref[rs, cs].astype(F32) * s * gb_ref[rs, cs].astype(F32)
            mix_ref[rs, w + g * hd:w + (g + 1) * hd] = yb.astype(BF16)

    o_ref[...] = x_ref[...] + jnp.dot(mix_ref[...], wout_ref[...], preferred_element_type=F32)


def _even_out(y, ga, u, v, gb, x2, w_glu, b_glu, w_sp, b_sp_t, w_out, *, tm):
    n_tok, d = x2.shape
    t, _, w = y.shape
    tok = lambda width: pl.BlockSpec((tm, width), lambda i: (i, 0))
    full = lambda a: pl.BlockSpec(a.shape, lambda i: (0,) * a.ndim)
    return pl.pallas_call(
        functools.partial(_even_out_body, tm=tm),
        grid=(n_tok // tm,),
        in_specs=[pl.BlockSpec((t, tm // t, w), lambda i: (0, i, 0)), tok(w), tok(w), tok(w), tok(w), tok(d),
                  full(w_glu), full(b_glu), full(w_sp), full(b_sp_t), full(w_out)],
        out_specs=tok(d),
        out_shape=jax.ShapeDtypeStruct((n_tok, d), F32),
        scratch_shapes=[pltpu.VMEM((tm, 2 * w), BF16), pltpu.VMEM((w // 128, tm, 128), F32)],
        compiler_params=_cparams("parallel"),
        name="even_out",
    )(y, ga, u, v, gb, x2, w_glu, b_glu, w_sp, b_sp_t, w_out)


def _attn_body(lq1_ref, lk1_ref, lq2_ref, lk2_ref, sg_ref, q_ref, k_ref, v_ref, o_ref, *scratch,
               tq, lambda_init):
    lam = (jnp.exp(jnp.sum(lq1_ref[...] * lk1_ref[...])) - jnp.exp(jnp.sum(lq2_ref[...] * lk2_ref[...]))
           + lambda_init)

    def q_tile(qi, c):
        _attn_tile(qi, lam, sg_ref, q_ref, k_ref, v_ref, o_ref, *scratch, tq=tq, lambda_init=lambda_init)
        return c

    lax.fori_loop(0, q_ref.shape[1] // tq, q_tile, 0)


def _attn_tile(qi, lam, sg_ref, q_ref, k_ref, v_ref, o_ref, s_ref, mx_ref, mrep_ref, ls_ref, acc_ref, *,
               tq, lambda_init):
    hd = DA_HEAD_DIM
    n_lane = tq // 128
    q0 = pl.multiple_of(qi * tq, tq)
    q = q_ref[0, pl.ds(q0, tq), :]
    qs = (q[:, 0:hd], q[:, hd:2 * hd])
    mx_ref[...] = jnp.full(mx_ref.shape, NEG_BIG, F32)
    ls_ref[...] = jnp.zeros(ls_ref.shape, F32)
    acc_ref[...] = jnp.zeros(acc_ref.shape, F32)

    def lane_fold(x, op):
        out = x[:, 0:128]
        for c in range(1, n_lane):
            out = op(out, x[:, c * 128:(c + 1) * 128])
        return out

    def scores(j, masked):
        k0 = pl.multiple_of(j * tq, tq)
        kj = k_ref[0, pl.ds(k0, tq), :]
        for m in range(2):
            s = lax.dot_general(qs[m], kj[:, m * hd:(m + 1) * hd], (((1,), (1,)), ((), ())),
                                preferred_element_type=F32)
            if masked:
                keep = (lax.broadcasted_iota(jnp.int32, (tq, tq), 1)
                        <= lax.broadcasted_iota(jnp.int32, (tq, tq), 0))
                s = jnp.where(keep, s, NEG_BIG)
            s_ref[m, j] = s
            mx_ref[m] = jnp.maximum(mx_ref[m], lane_fold(s, jnp.maximum))

    def run_blocks(n_blk, fn):
        def quad(jg, c):
            for u in range(4):
                fn(4 * jg + u)
            return c

        lax.fori_loop(0, n_blk // 4, quad, 0)

        @pl.when(n_blk % 4 >= 2)
        def _():
            base = (n_blk // 4) * 4
            fn(base)
            fn(base + 1)

        @pl.when(n_blk % 2 == 1)
        def _():
            fn(n_blk - 1)

    run_blocks(qi, functools.partial(scores, masked=False))
    scores(qi, True)

    for m in range(2):
        mrep_ref[m] = jnp.broadcast_to(jnp.max(mx_ref[m], axis=-1, keepdims=True), (tq, 128))

    def accumulate(j):
        k0 = pl.multiple_of(j * tq, tq)
        vj = v_ref[0, pl.ds(k0, tq), :]
        for m in range(2):
            s = s_ref[m, j]
            mrep = mrep_ref[m]
            p = jnp.concatenate([jnp.exp2(s[:, c_ * 128:(c_ + 1) * 128] - mrep) for c_ in range(n_lane)],
                                axis=1)
            ls_ref[m] += lane_fold(p, jnp.add)
            acc_ref[m] += jnp.dot(p.astype(BF16), vj, preferred_element_type=F32)

    run_blocks(qi + 1, accumulate)

    o1 = acc_ref[0] / jnp.sum(ls_ref[0], axis=-1, keepdims=True)
    o2 = acc_ref[1] / jnp.sum(ls_ref[1], axis=-1, keepdims=True)
    o = o1 - lam * o2
    ms = jnp.mean(o * o, axis=-1, keepdims=True)
    o = (o * lax.rsqrt(ms + RMS_EPS)) * sg_ref[...]
    o_ref[0, pl.ds(q0, tq), :] = (o * (1.0 - lambda_init)).astype(o_ref.dtype)


def _diff_attn(q, k, v, lq1, lk1, lq2, lk2, subln_g, lambda_init, *, tq):
    b, seq, width = q.shape
    vec = lambda a: a.reshape(1, -1)
    small = lambda n: pl.BlockSpec((1, n), lambda bi, hi: (0, 0))
    head = pl.BlockSpec((1, seq, DA_V_DIM), lambda bi, hi: (bi, 0, hi))
    scratch = [pltpu.VMEM((2, seq // tq, tq, tq), F32),
               pltpu.VMEM((2, tq, 128), F32),
               pltpu.VMEM((2, tq, 128), F32),
               pltpu.VMEM((2, tq, 128), F32),
               pltpu.VMEM((2, tq, DA_V_DIM), F32)]
    return pl.pallas_call(
        functools.partial(_attn_body, tq=tq, lambda_init=lambda_init),
        grid=(b, DA_HEADS),
        in_specs=[small(DA_HEAD_DIM)] * 4 + [small(DA_V_DIM), head, head, head],
        out_specs=head,
        out_shape=jax.ShapeDtypeStruct((b, seq, width), BF16),
        scratch_shapes=scratch,
        compiler_params=_cparams("parallel", "parallel"),
        name="diff_attn",
    )(vec(lq1), vec(lk1), vec(lq2), vec(lk2), vec(subln_g), q, k, v)


def _odd_out_body(o_ref, g_ref, x_ref, w_ref, fn_ref, out_ref, *, final_norm):
    gated = (o_ref[...].astype(F32) * g_ref[...].astype(F32)).astype(BF16)
    y = x_ref[...] + jnp.dot(gated, w_ref[...], preferred_element_type=F32)
    if final_norm:
        ms = jnp.mean(y * y, axis=-1, keepdims=True)
        y = (y * lax.rsqrt(ms + RMS_EPS)) * fn_ref[...]
    out_ref[...] = y


def _odd_out(o, g, x2, w_out, final_g, *, tm, final_norm):
    n_tok, d = x2.shape
    tok = lambda dt: pl.BlockSpec((tm, d), lambda i: (i, 0))
    return pl.pallas_call(
        functools.partial(_odd_out_body, final_norm=final_norm),
        grid=(n_tok // tm,),
        in_specs=[tok(BF16), tok(BF16), tok(F32),
                  pl.BlockSpec(w_out.shape, lambda i: (0, 0)),
                  pl.BlockSpec((1, d), lambda i: (0, 0))],
        out_specs=tok(F32),
        out_shape=jax.ShapeDtypeStruct((n_tok, d), F32),
        compiler_params=_cparams("parallel"),
        name="odd_out",
    )(o, g, x2, w_out, final_g.reshape(1, d))


def _even_layer(x2, bsz, seq, norm_g, w_in, lam_re, lam_im, log_dt, b_re, b_im, c_re, c_im, d_skip,
                w_glu, b_glu, ln_g, ln_b, w_sp, b_sp, w_out):
    n_tok, d = x2.shape
    w = d // 2
    n_chunks = seq // SSM_CHUNK
    n_pow = max(1, (n_chunks - 1).bit_length())

    full = lambda a: pl.BlockSpec(a.shape, lambda i, j: (0, 0))
    ln_g2, ln_b2 = ln_g.reshape(1, w), ln_b.reshape(1, w)
    xa_cm, ga, u, v, gb = _in_proj(
        x2, norm_g, w_in.astype(BF16),
        [(_ep_identity, IN_PROJ_COL_CHUNK), (_ep_silu, IN_PROJ_COL_CHUNK), (_ep_gelu, IN_PROJ_COL_CHUNK),
         (_ep_gelu_layernorm, None), (_ep_silu, IN_PROJ_COL_CHUNK)],
        [BF16] * 5, aux=(ln_g2, ln_b2), aux_specs=(full(ln_g2), full(ln_b2)),
        tm=512, tn=w, name="even_in_proj", chunk_major_first=True, resident=True)

    bb_re, bb_im, a, pw = _s5_prep(lam_re, lam_im, log_dt, b_re, b_im, n_pow)
    ops = _s5_operators(bb_re, bb_im, a, pw, c_re, c_im, d_skip, n_pow)
    y = _s5(xa_cm, *ops, bsz, n_pow)

    return _even_out(y, ga, u, v, gb, x2, w_glu.astype(BF16), b_glu.reshape(1, w),
                     w_sp, b_sp.T, w_out.astype(BF16), tm=512)


def _odd_layer(x2, bsz, seq, norm_g, w_in, lq1, lk1, lq2, lk2, subln_g, w_out, lambda_init,
               final_g, final_norm):
    n_tok, d = x2.shape
    tm = 256
    tab = _rope_table(seq)
    n_pos_blocks = seq // tm
    tab_spec = pl.BlockSpec((tm, tab.shape[1]), lambda i, j: (i % n_pos_blocks, 0))
    q, k, v, g = _in_proj(
        x2, norm_g, w_in.astype(BF16),
        [(_ep_rope_scaled, IN_PROJ_COL_CHUNK), (_ep_rope, IN_PROJ_COL_CHUNK),
         (_ep_identity, IN_PROJ_COL_CHUNK), (_ep_silu, IN_PROJ_COL_CHUNK)],
        [BF16] * 4, aux=(tab,), aux_specs=(tab_spec,), tm=tm, tn=d, name="odd_in_proj", resident=True)
    shp = (bsz, seq, d)
    o = _diff_attn(q.reshape(shp), k.reshape(shp), v.reshape(shp), lq1, lk1, lq2, lk2, subln_g,
                   lambda_init, tq=ATTN_TILE)
    return _odd_out(o.reshape(n_tok, d), g, x2, w_out.astype(BF16), final_g, tm=512,
                    final_norm=final_norm)


def kernel(x, ev_norm, ev_w_in, ssm_lam_re, ssm_lam_im, ssm_log_dt, ssm_b_re, ssm_b_im, ssm_c_re, ssm_c_im, ssm_d, ssm_w_glu, ssm_b_glu, sg_ln_g, sg_ln_b, sg_w_sp, sg_b_sp, ev_w_out, od_norm, od_w_in, da_lq1, da_lk1, da_lq2, da_lk2, da_subln, od_w_out, final_norm):
    bsz, seq, d = x.shape
    depth = ev_norm.shape[0] + od_norm.shape[0]
    assert depth % 2 == 0, "the final rmsnorm is fused into the last (odd) layer"
    x2 = x.reshape(bsz * seq, d)
    for i in range(depth):
        j = i // 2
        if i % 2 == 0:
            x2 = _even_layer(x2, bsz, seq, ev_norm[j], ev_w_in[j], ssm_lam_re[j], ssm_lam_im[j],
                             ssm_log_dt[j], ssm_b_re[j], ssm_b_im[j], ssm_c_re[j], ssm_c_im[j],
                             ssm_d[j], ssm_w_glu[j], ssm_b_glu[j], sg_ln_g[j], sg_ln_b[j],
                             sg_w_sp[j], sg_b_sp[j], ev_w_out[j])
        else:
            lambda_init = 0.8 - 0.6 * math.exp(-0.3 * i)
            x2 = _odd_layer(x2, bsz, seq, od_norm[j], od_w_in[j], da_lq1[j], da_lk1[j], da_lq2[j],
                            da_lk2[j], da_subln[j], od_w_out[j], lambda_init, final_norm,
                            final_norm=(i == depth - 1))
    return x2.reshape(bsz, seq, d)
```
